```python
import math
import jax
import jax.numpy as jnp
from jax import lax
import numpy as np

D_MODEL = 4096
BATCH = 2
SEQ = 8192
DEPTH = 1

CTX_LEN = 256
GRID_W = 64
HEAD_DIM = 128
N_HEADS = 16
N_KV_HEADS = 4
GQA_GROUP = N_HEADS // N_KV_HEADS
WINDOW = 128
BLOCK = 128
ROPE_BASE = 10000.0
ROPE_PAIRS = HEAD_DIM // 4
SSM_HEAD_DIM = 64
SSM_INNER = D_MODEL // 2
SSM_HEADS = SSM_INNER // SSM_HEAD_DIM
SSM_GROUPS = 4
SSM_HEADS_PER_GROUP = SSM_HEADS // SSM_GROUPS
SSM_STATE = 128
CONV_K = 5
CHUNK = 128
ATTN_WIDTH = N_HEADS * HEAD_DIM
MIX_WIDTH = ATTN_WIDTH + SSM_INNER
KV_COLS = N_KV_HEADS * HEAD_DIM
BC_COLS = SSM_GROUPS * SSM_STATE
XBC_COLS = SSM_INNER + 2 * BC_COLS
DT_COLS = 2 * SSM_HEADS
CTX_COL0 = ATTN_WIDTH + SSM_INNER
IN_COLS = CTX_COL0 + 2 * KV_COLS + XBC_COLS + DT_COLS
D_FF = 128 * ((8 * D_MODEL // 3 + 127) // 128)
N_MOD = 9
EPS = 1e-6
NEG_INF = -1e30

kernel_name = 'hymba_swa_ssd_macaron_dit_block'


def rmsnorm(t, g):
    tf = t.astype(jnp.float32)
    return (tf * lax.rsqrt(jnp.mean(tf * tf, axis=-1, keepdims=True) + EPS) * g).astype(t.dtype)


def modulate(t, shift, scale):
    return t * (1 + scale[:, None]) + shift[:, None]


def add_residual(s, y, g_post, gate, coef):
    return s + coef * gate[:, None] * rmsnorm(y, g_post)


def adaln(cv, w_mod, b_mod):
    return (jax.nn.silu(cv) @ w_mod + b_mod).reshape(cv.shape[0], N_MOD, D_MODEL)


def swiglu(h, wg, wu, wd):
    return (jax.nn.silu(h @ wg) * (h @ wu)) @ wd


def ffn_half(s, mod, slot, g_pre, g_post, wg, wu, wd):
    h = modulate(rmsnorm(s, g_pre), mod[:, 3 * slot], mod[:, 3 * slot + 1])
    return add_residual(s, swiglu(h, wg, wu, wd), g_post, mod[:, 3 * slot + 2], 0.5)


def rope_tables(rows_count):
    row = jnp.repeat(jnp.arange(rows_count), GRID_W).astype(jnp.float32)
    col = jnp.tile(jnp.arange(GRID_W), rows_count).astype(jnp.float32)
    inv = ROPE_BASE ** (-jnp.arange(ROPE_PAIRS, dtype=jnp.float32) / ROPE_PAIRS)
    ar = row[:, None] * inv
    ac = col[:, None] * inv
    ang = jnp.concatenate([ar, ar, ac, ac], axis=-1)
    return jnp.cos(ang), jnp.sin(ang)


def apply_rope(t, cos, sin):
    tf = t.astype(jnp.float32)
    seg = tf.reshape(*tf.shape[:-1], 2, 2, ROPE_PAIRS)
    rot = jnp.concatenate([-seg[..., 1:, :], seg[..., :1, :]], axis=-2).reshape(tf.shape)
    return (tf * cos[None, :, None] + rot * sin[None, :, None]).astype(t.dtype)


def sink_softmax(scores, sink):
    m = sink
    for s in scores:
        m = jnp.maximum(m, s.max(axis=-1, keepdims=True))
    ps = [jnp.exp(s - m) for s in scores]
    denom = jnp.exp(sink - m) + sum(p.sum(axis=-1, keepdims=True) for p in ps)
    return [p / denom for p in ps]


def window_attention(q, k, v, k_ctx, v_ctx, sink):
    bsz, n = q.shape[:2]
    nb = n // BLOCK
    scale = HEAD_DIM ** -0.5
    qb = q.reshape(bsz, nb, BLOCK, N_KV_HEADS, GQA_GROUP, HEAD_DIM)

    def band(t):
        tp = jnp.pad(t, ((0, 0), (BLOCK, BLOCK), (0, 0), (0, 0)))
        tp = tp.reshape(bsz, nb + 2, BLOCK, N_KV_HEADS, HEAD_DIM)
        return jnp.concatenate([tp[:, :-2], tp[:, 1:-1], tp[:, 2:]], axis=2)

    kb, vb = band(k), band(v)
    s_band = jnp.einsum('bnqkgd,bnskd->bnkgqs', qb, kb, preferred_element_type=jnp.float32) * scale
    s_ctx = jnp.einsum('bnqkgd,bskd->bnkgqs', qb, k_ctx, preferred_element_type=jnp.float32) * scale
    qi = jnp.arange(BLOCK)[:, None]
    kj = jnp.arange(3 * BLOCK)[None, :]
    in_window = jnp.abs(kj - BLOCK - qi) <= WINDOW
    key_pos = (jnp.arange(nb)[:, None] - 1) * BLOCK + jnp.arange(3 * BLOCK)[None, :]
    in_range = (key_pos >= 0) & (key_pos < n)
    mask = in_window[None] & in_range[:, None, :]
    s_band = jnp.where(mask[None, :, None, None], s_band, NEG_INF)
    sink_l = sink.astype(jnp.float32).reshape(1, 1, N_KV_HEADS, GQA_GROUP, 1, 1)
    p_band, p_ctx = sink_softmax([s_band, s_ctx], sink_l)
    o = (jnp.einsum('bnkgqs,bnskd->bnqkgd', p_band.astype(v.dtype), vb)
         + jnp.einsum('bnkgqs,bskd->bnqkgd', p_ctx.astype(v.dtype), v_ctx))
    return o.reshape(bsz, n, ATTN_WIDTH)


def context_attention(q, k, v, sink):
    bsz, lc = q.shape[:2]
    qg = q.reshape(bsz, lc, N_KV_HEADS, GQA_GROUP, HEAD_DIM)
    s = jnp.einsum('bqkgd,bskd->bkgqs', qg, k, preferred_element_type=jnp.float32) * HEAD_DIM ** -0.5
    (p,) = sink_softmax([s], sink.astype(jnp.float32).reshape(1, N_KV_HEADS, GQA_GROUP, 1, 1))
    o = jnp.einsum('bkgqs,bskd->bqkgd', p.astype(v.dtype), v)
    return o.reshape(bsz, lc, ATTN_WIDTH)


def depthwise_conv(u, w, b):
    out = lax.conv_general_dilated(
        u, w[:, None, :].astype(u.dtype), window_strides=(1,),
        padding=[(CONV_K // 2, CONV_K // 2)], dimension_numbers=('NWC', 'WIO', 'NWC'),
        feature_group_count=u.shape[-1])
    return out + b


def split_tail(p):
    bsz, L = p.shape[:2]
    k = p[..., :KV_COLS].reshape(bsz, L, N_KV_HEADS, HEAD_DIM)
    v = p[..., KV_COLS:2 * KV_COLS].reshape(bsz, L, N_KV_HEADS, HEAD_DIM)
    xbc = p[..., 2 * KV_COLS:2 * KV_COLS + XBC_COLS]
    dt_raw = p[..., 2 * KV_COLS + XBC_COLS:]
    return k, v, xbc, dt_raw


def split_ssm(xbc, dt_raw, dt_bias):
    bsz, L = xbc.shape[:2]
    xs = xbc[..., :SSM_INNER].reshape(bsz, L, SSM_GROUPS, SSM_HEADS_PER_GROUP, SSM_HEAD_DIM)
    bs = xbc[..., SSM_INNER:SSM_INNER + BC_COLS].reshape(bsz, L, SSM_GROUPS, SSM_STATE)
    cs = xbc[..., SSM_INNER + BC_COLS:].reshape(bsz, L, SSM_GROUPS, SSM_STATE)
    dt = jax.nn.softplus(
        dt_raw.astype(jnp.float32).reshape(bsz, L, 2, SSM_GROUPS, SSM_HEADS_PER_GROUP)
        + dt_bias.astype(jnp.float32).reshape(2, SSM_GROUPS, SSM_HEADS_PER_GROUP))
    return xs, bs, cs, dt[:, :, 0], dt[:, :, 1]


def ssd_scan(xs, dt, a_log, bs, cs, h0, with_y):
    bsz, n = xs.shape[:2]
    nc = n // CHUNK
    a = -jnp.exp(a_log.astype(jnp.float32))
    acs = jnp.cumsum((dt * a).reshape(bsz, nc, CHUNK, SSM_GROUPS, SSM_HEADS_PER_GROUP), axis=2)
    a_tot = acs[:, :, -1]
    xdt = (xs.astype(jnp.float32) * dt[..., None]).reshape(
        bsz, nc, CHUNK, SSM_GROUPS, SSM_HEADS_PER_GROUP, SSM_HEAD_DIM)
    bch = bs.astype(jnp.float32).reshape(bsz, nc, CHUNK, SSM_GROUPS, SSM_STATE)
    states = jnp.einsum('bcqgn,bcqgr,bcqgrp->bcgrpn', bch, jnp.exp(a_tot[:, :, None] - acs), xdt)

    def step(h, inp):
        decay, st = inp
        return decay[..., None, None] * h + st, (h if with_y else None)

    h_final, h_prev = lax.scan(step, h0, (jnp.moveaxis(jnp.exp(a_tot), 1, 0), jnp.moveaxis(states, 1, 0)))
    if not with_y:
        return None, h_final
    h_prev = jnp.moveaxis(h_prev, 0, 1)
    cch = cs.astype(jnp.float32).reshape(bsz, nc, CHUNK, SSM_GROUPS, SSM_STATE)
    y_off = jnp.einsum('bcqgn,bcgrpn->bcqgrp', cch, h_prev) * jnp.exp(acs)[..., None]
    tril = jnp.tril(jnp.ones((CHUNK, CHUNK), dtype=bool))
    seg = acs[:, :, :, None] - acs[:, :, None]
    decay_ij = jnp.exp(jnp.where(tril[:, :, None, None], seg, -jnp.inf))
    cb = jnp.einsum('bcign,bcjgn->bcijg', cch, bch)
    y_diag = jnp.einsum('bcijgr,bcjgrp->bcigrp', cb[..., None] * decay_ij, xdt)
    return (y_diag + y_off).reshape(bsz, n, SSM_GROUPS, SSM_HEADS_PER_GROUP, SSM_HEAD_DIM), h_final


def flip(t):
    return jnp.flip(t, axis=1)


def bidirectional_ssd(lat, ctx, a_log, d_skip, with_ctx_y):
    xs, bs, cs, dtf, dtb = lat
    xc, bc, cc, dtcf, dtcb = ctx
    bsz = xs.shape[0]
    h0 = jnp.zeros((bsz, SSM_GROUPS, SSM_HEADS_PER_GROUP, SSM_HEAD_DIM, SSM_STATE), jnp.float32)
    a_f = a_log[0].reshape(SSM_GROUPS, SSM_HEADS_PER_GROUP)
    a_b = a_log[1].reshape(SSM_GROUPS, SSM_HEADS_PER_GROUP)
    yc_f, hc_f = ssd_scan(xc, dtcf, a_f, bc, cc, h0, with_ctx_y)
    yc_b, hc_b = ssd_scan(flip(xc), flip(dtcb), a_b, flip(bc), flip(cc), h0, with_ctx_y)
    y_f, _ = ssd_scan(xs, dtf, a_f, bs, cs, hc_f, True)
    y_b, _ = ssd_scan(flip(xs), flip(dtb), a_b, flip(bs), flip(cs), hc_b, True)
    d = d_skip.astype(jnp.float32).reshape(SSM_GROUPS, SSM_HEADS_PER_GROUP, 1)
    y_lat = (y_f + flip(y_b) + d * xs).reshape(bsz, xs.shape[1], SSM_INNER)
    if not with_ctx_y:
        return y_lat, None
    y_ctx = (yc_f + flip(yc_b) + d * xc).reshape(bsz, xc.shape[1], SSM_INNER)
    return y_lat, y_ctx


def merge_groups(attn, y_ssm, z, attn_norm, ssm_norm, w_out):
    a = rmsnorm(attn, attn_norm)
    s = rmsnorm(y_ssm.astype(z.dtype) * jax.nn.silu(z), ssm_norm)
    return jnp.concatenate([a, s], axis=-1) @ w_out


def hybrid_mix(h_lat, h_ctx, cos, sin, w_in, sink, attn_norm, conv_w, conv_b, a_log, dt_bias,
               d_skip, ssm_norm, w_out, with_ctx_out):
    bsz, n = h_lat.shape[:2]
    lc = h_ctx.shape[1]
    p = h_lat @ w_in
    q = apply_rope(p[..., :ATTN_WIDTH].reshape(bsz, n, N_HEADS, HEAD_DIM), cos, sin)
    z = p[..., ATTN_WIDTH:CTX_COL0]
    k, v, xbc, dt_raw = split_tail(p[..., CTX_COL0:])
    k = apply_rope(k, cos, sin)
    kc, vc, xbcc, dtc = split_tail(h_ctx @ w_in[:, CTX_COL0:])
    attn = window_attention(q, k, v, kc, vc, sink)
    lat_ssm = split_ssm(jax.nn.silu(depthwise_conv(xbc, conv_w, conv_b)), dt_raw, dt_bias)
    ctx_ssm = split_ssm(jax.nn.silu(depthwise_conv(xbcc, conv_w, conv_b)), dtc, dt_bias)
    y_lat, y_ctx = bidirectional_ssd(lat_ssm, ctx_ssm, a_log, d_skip, with_ctx_out)
    out_lat = merge_groups(attn, y_lat, z, attn_norm, ssm_norm, w_out)
    if not with_ctx_out:
        return out_lat, None
    pc_head = h_ctx @ w_in[:, :CTX_COL0]
    qc = pc_head[..., :ATTN_WIDTH].reshape(bsz, lc, N_HEADS, HEAD_DIM)
    attn_c = context_attention(qc, kc, vc, sink)
    out_ctx = merge_groups(attn_c, y_ctx, pc_head[..., ATTN_WIDTH:], attn_norm, ssm_norm, w_out)
    return out_lat, out_ctx


def setup_inputs(seed: int = 0) -> dict:
    key = jax.random.key(seed)
    ks = jax.random.split(key, 21)
    f32 = jnp.float32

    def normal(k, shape, scale):
        return jax.random.normal(k, shape, f32) * scale

    dt0 = jnp.exp(jax.random.uniform(ks[17], (DEPTH, 2, SSM_HEADS), f32, math.log(1e-3), math.log(1e-1)))
    return {
        'x': normal(ks[0], (BATCH, SEQ, D_MODEL), 1.0),
        'c': normal(ks[1], (BATCH, D_MODEL), 1.0),
        'ctx': normal(ks[2], (BATCH, CTX_LEN, D_MODEL), 1.0),
        'c_ctx': normal(ks[3], (D_MODEL,), 1.0),
        'w_mod': normal(ks[4], (DEPTH, D_MODEL, N_MOD * D_MODEL), 0.5 * D_MODEL ** -0.5),
        'b_mod': normal(ks[5], (DEPTH, N_MOD * D_MODEL), 0.02),
        'norm_pre': 1.0 + normal(ks[6], (DEPTH, 3, D_MODEL), 0.1),
        'norm_post': 1.0 + normal(ks[7], (DEPTH, 3, D_MODEL), 0.1),
        'w_ffn_gate': normal(ks[8], (DEPTH, 2, D_MODEL, D_FF), D_MODEL ** -0.5),
        'w_ffn_up': normal(ks[9], (DEPTH, 2, D_MODEL, D_FF), D_MODEL ** -0.5),
        'w_ffn_down': normal(ks[10], (DEPTH, 2, D_FF, D_MODEL), D_FF ** -0.5),
        'w_in': normal(ks[11], (DEPTH, D_MODEL, IN_COLS), D_MODEL ** -0.5),
        'attn_sink': normal(ks[12], (DEPTH, N_HEADS), 1.0),
        'attn_norm': 1.0 + normal(ks[13], (DEPTH, ATTN_WIDTH), 0.1),
        'conv_w': normal(ks[14], (DEPTH, CONV_K, XBC_COLS), CONV_K ** -0.5),
        'conv_b': normal(ks[15], (DEPTH, XBC_COLS), 0.02),
        'a_log': jnp.log(jax.random.uniform(ks[16], (DEPTH, 2, SSM_HEADS), f32, 1.0, 16.0)),
        'dt_bias': dt0 + jnp.log(-jnp.expm1(-dt0)),
        'd_skip': 1.0 + normal(ks[18], (DEPTH, SSM_HEADS), 0.1),
        'ssm_norm': 1.0 + normal(ks[19], (DEPTH, SSM_INNER), 0.1),
        'w_out': normal(ks[20], (DEPTH, MIX_WIDTH, D_MODEL), MIX_WIDTH ** -0.5),
    }


def reference(x, c, ctx, c_ctx, w_mod, b_mod, norm_pre, norm_post, w_ffn_gate, w_ffn_up, w_ffn_down,
              w_in, attn_sink, attn_norm, conv_w, conv_b, a_log, dt_bias, d_skip, ssm_norm, w_out):
    ROWS = x.shape[1] // GRID_W
    cos, sin = rope_tables(ROWS)
    for layer in range(DEPTH):
        last = layer == DEPTH - 1
        mod_x = adaln(c, w_mod[layer], b_mod[layer])
        mod_c = adaln(c_ctx[None], w_mod[layer], b_mod[layer])
        ffn0 = (norm_pre[layer, 0], norm_post[layer, 0], w_ffn_gate[layer, 0], w_ffn_up[layer, 0], w_ffn_down[layer, 0])
        x = ffn_half(x, mod_x, 0, *ffn0)
        ctx = ffn_half(ctx, mod_c, 0, *ffn0)
        hx = modulate(rmsnorm(x, norm_pre[layer, 1]), mod_x[:, 3], mod_x[:, 4])
        hc = modulate(rmsnorm(ctx, norm_pre[layer, 1]), mod_c[:, 3], mod_c[:, 4])
        y_x, y_c = hybrid_mix(hx, hc, cos, sin, w_in[layer], attn_sink[layer], attn_norm[layer],
                              conv_w[layer], conv_b[layer], a_log[layer], dt_bias[layer], d_skip[layer],
                              ssm_norm[layer], w_out[layer], not last)
        x = add_residual(x, y_x, norm_post[layer, 1], mod_x[:, 5], 1.0)
        ffn1 = (norm_pre[layer, 2], norm_post[layer, 2], w_ffn_gate[layer, 1], w_ffn_up[layer, 1], w_ffn_down[layer, 1])
        x = ffn_half(x, mod_x, 2, *ffn1)
        if not last:
            ctx = add_residual(ctx, y_c, norm_post[layer, 1], mod_c[:, 5], 1.0)
            ctx = ffn_half(ctx, mod_c, 2, *ffn1)
    return x
```

```python
import functools

import jax
import jax.numpy as jnp
from jax import lax
from jax.experimental import pallas as pl
from jax.experimental.pallas import tpu as pltpu

GRID_W = 64
HEAD_DIM = 128
N_HEADS = 16
N_KV_HEADS = 4
GQA_GROUP = N_HEADS // N_KV_HEADS
BLOCK = 128
ROPE_BASE = 10000.0
ROPE_PAIRS = HEAD_DIM // 4
SSM_HEAD_DIM = 64
SSM_GROUPS = 4
SSM_STATE = 128
CONV_K = 5
CHUNK = 128
N_MOD = 9
EPS = 1e-6
NEG_INF = -1e30

LANES = 128
SUBLANES = 8
MIB = 1024 * 1024

F32 = jnp.float32
BF16 = jnp.bfloat16


def _params(semantics, vmem_mib):
    return pltpu.CompilerParams(dimension_semantics=semantics, vmem_limit_bytes=vmem_mib * MIB)


def _silu(t):
    return t * jax.nn.sigmoid(t)


def _rms_scale(t):
    return lax.rsqrt(jnp.mean(t * t, axis=-1, keepdims=True) + EPS)


def _mod_kernel(c_ref, w_ref, b_ref, o_ref):
    s = _silu(c_ref[...]).astype(BF16)
    o_ref[...] = jnp.dot(s, w_ref[...].astype(BF16), preferred_element_type=F32) + b_ref[...]


def _adaln(cv, w_mod, b_mod):
    rows, d = cv.shape
    n = w_mod.shape[1]
    tn = 512
    return pl.pallas_call(
        _mod_kernel,
        grid=(n // tn,),
        in_specs=[
            pl.BlockSpec((rows, d), lambda j: (0, 0)),
            pl.BlockSpec((d, tn), lambda j: (0, j)),
            pl.BlockSpec((1, tn), lambda j: (0, j)),
        ],
        out_specs=pl.BlockSpec((rows, tn), lambda j: (0, j)),
        out_shape=jax.ShapeDtypeStruct((rows, n), F32),
        compiler_params=_params(("parallel",), 40),
        name="adaln",
    )(cv, w_mod, b_mod.reshape(1, n))


def _prenorm_kernel(s_ref, g_ref, sh_ref, sc_ref, h_ref):
    s = s_ref[0]
    h = s * _rms_scale(s) * g_ref[...]
    h_ref[0] = (h * (1.0 + sc_ref[0]) + sh_ref[0]).astype(BF16)


def _prenorm(s, g_pre, shift, scale, tr=256):
    b, n, d = s.shape
    row = pl.BlockSpec((1, tr, d), lambda i, r: (i, r, 0))
    vec = pl.BlockSpec((1, 1, d), lambda i, r: (i, 0, 0))
    return pl.pallas_call(
        _prenorm_kernel,
        grid=(b, n // tr),
        in_specs=[row, pl.BlockSpec((1, d), lambda i, r: (0, 0)), vec, vec],
        out_specs=row,
        out_shape=jax.ShapeDtypeStruct((b, n, d), BF16),
        compiler_params=_params(("parallel", "parallel"), 32),
        name="prenorm",
    )(s, g_pre.reshape(1, d), shift.reshape(b, 1, d), scale.reshape(b, 1, d))


def _resid_kernel(*refs, coef, with_h):
    if with_h:
        s_ref, y_ref, gp_ref, gate_ref, gn_ref, sh_ref, sc_ref, o_ref, h_ref = refs
    else:
        s_ref, y_ref, gp_ref, gate_ref, o_ref = refs
    y = y_ref[0]
    s_new = s_ref[0] + (coef * gate_ref[0]) * (y * _rms_scale(y) * gp_ref[...])
    o_ref[0] = s_new
    if with_h:
        h = s_new * _rms_scale(s_new) * gn_ref[...]
        h_ref[0] = (h * (1.0 + sc_ref[0]) + sh_ref[0]).astype(BF16)


def _resid(s, y, g_post, gate, coef, nxt=None, tr=256):
    b, n, d = s.shape
    row = pl.BlockSpec((1, tr, d), lambda i, r: (i, r, 0))
    vec = pl.BlockSpec((1, 1, d), lambda i, r: (i, 0, 0))
    par = pl.BlockSpec((1, d), lambda i, r: (0, 0))
    args = [s, y.reshape(b, n, d), g_post.reshape(1, d), gate.reshape(b, 1, d)]
    in_specs = [row, row, par, vec]
    out_shape = [jax.ShapeDtypeStruct((b, n, d), F32)]
    out_specs = [row]
    if nxt is not None:
        g_next, shift, scale = nxt
        args += [g_next.reshape(1, d), shift.reshape(b, 1, d), scale.reshape(b, 1, d)]
        in_specs += [par, vec, vec]
        out_shape.append(jax.ShapeDtypeStruct((b, n, d), BF16))
        out_specs.append(row)
    out = pl.pallas_call(
        functools.partial(_resid_kernel, coef=coef, with_h=nxt is not None),
        grid=(b, n // tr),
        in_specs=in_specs,
        out_specs=out_specs,
        out_shape=out_shape,
        compiler_params=_params(("parallel", "parallel"), 48),
        name="resid",
    )(*args)
    return out if nxt is not None else out[0]


def _ffn_kernel(h_ref, wg_ref, wu_ref, wd_ref, y_ref, *, n_chunk):
    @pl.when(pl.program_id(1) == 0)
    def _():
        y_ref[...] = jnp.zeros_like(y_ref)

    h = h_ref[...]
    g = jnp.dot(h, wg_ref[...], preferred_element_type=F32)
    u = jnp.dot(h, wu_ref[...], preferred_element_type=F32)
    a = (_silu(g) * u).astype(BF16)
    d = y_ref.shape[1]
    for n0 in range(0, d, n_chunk):
        y_ref[:, n0:n0 + n_chunk] += jnp.dot(a, wd_ref[:, n0:n0 + n_chunk], preferred_element_type=F32)


def _ffn(h, wg, wu, wd, tm=512, tf=256):
    r, d = h.shape
    f = wg.shape[1]
    tm = min(tm, r)
    return pl.pallas_call(
        functools.partial(_ffn_kernel, n_chunk=1024),
        grid=(r // tm, f // tf),
        in_specs=[
            pl.BlockSpec((tm, d), lambda i, j: (i, 0)),
            pl.BlockSpec((d, tf), lambda i, j: (0, j)),
            pl.BlockSpec((d, tf), lambda i, j: (0, j)),
            pl.BlockSpec((tf, d), lambda i, j: (j, 0)),
        ],
        out_specs=pl.BlockSpec((tm, d), lambda i, j: (i, 0)),
        out_shape=jax.ShapeDtypeStruct((r, d), F32),
        compiler_params=_params(("parallel", "arbitrary"), 52),
        name="ffn",
    )(h, wg, wu, wd)


def _proj_kernel(*refs, rope_tiles):
    if rope_tiles:
        h_ref, w_ref, cos_ref, sa_ref, sb_ref, o_ref = refs
    else:
        h_ref, w_ref, o_ref = refs
    res = jnp.dot(h_ref[...], w_ref[...], preferred_element_type=F32)
    if not rope_tiles:
        o_ref[...] = res.astype(o_ref.dtype)
        return
    j = pl.program_id(1)
    is_rope = functools.reduce(jnp.logical_or, [(j >= lo) & (j < hi) for lo, hi in rope_tiles])

    @pl.when(is_rope)
    def _():
        cos, sa, sb = cos_ref[...], sa_ref[...], sb_ref[...]
        for k in range(res.shape[1] // HEAD_DIM):
            t = res[:, k * HEAD_DIM:(k + 1) * HEAD_DIM]
            o = t * cos + pltpu.roll(t, ROPE_PAIRS, 1) * sa + pltpu.roll(t, HEAD_DIM - ROPE_PAIRS, 1) * sb
            o_ref[:, k * HEAD_DIM:(k + 1) * HEAD_DIM] = o.astype(o_ref.dtype)

    @pl.when(jnp.logical_not(is_rope))
    def _():
        o_ref[...] = res.astype(o_ref.dtype)


def _proj(h, w, out_dtype, *, col0=0, n_cols=None, tm=512, tn=512, rope=None):
    r, kdim = h.shape
    n_cols = w.shape[1] - col0 if n_cols is None else n_cols
    tm = min(tm, r)
    tn = min(tn, n_cols)
    jo = col0 // tn
    in_specs = [
        pl.BlockSpec((tm, kdim), lambda i, j: (i, 0)),
        pl.BlockSpec((kdim, tn), lambda i, j: (0, j + jo)),
    ]
    args = [h, w]
    rope_tiles = None
    if rope is not None:
        tables, seq_len, rope_tiles = rope
        per_seq = seq_len // tm
        tab = pl.BlockSpec((tm, HEAD_DIM), lambda i, j: (i % per_seq, 0))
        in_specs += [tab, tab, tab]
        args += list(tables)
    return pl.pallas_call(
        functools.partial(_proj_kernel, rope_tiles=rope_tiles),
        grid=(r // tm, n_cols // tn),
        in_specs=in_specs,
        out_specs=pl.BlockSpec((tm, tn), lambda i, j: (i, j)),
        out_shape=jax.ShapeDtypeStruct((r, n_cols), out_dtype),
        compiler_params=_params(("parallel", "arbitrary"), 40),
        name="proj",
    )(*args)


def _rope_tables(n):
    row = (jnp.arange(n) // GRID_W).astype(F32)
    col = (jnp.arange(n) % GRID_W).astype(F32)
    inv = ROPE_BASE ** (-jnp.arange(ROPE_PAIRS, dtype=F32) / ROPE_PAIRS)
    ar = row[:, None] * inv
    ac = col[:, None] * inv
    ang = jnp.concatenate([ar, ar, ac, ac], axis=-1)
    cos, sin = jnp.cos(ang), jnp.sin(ang)
    upper = (jnp.arange(HEAD_DIM) % (2 * ROPE_PAIRS)) >= ROPE_PAIRS
    return cos, jnp.where(upper, sin, 0.0), jnp.where(upper, 0.0, -sin)


def _conv_kernel(x_ref, w_ref, b_ref, o_ref, pad_ref, *, rows):
    n = x_ref.shape[1]
    tc = x_ref.shape[2]
    halo = CONV_K // 2
    zeros = jnp.zeros((SUBLANES, tc), F32)
    pad_ref[0:SUBLANES, :] = zeros
    pad_ref[n + SUBLANES:n + 2 * SUBLANES, :] = zeros
    pad_ref[SUBLANES:n + SUBLANES, :] = x_ref[0]
    w = w_ref[...]
    bias = b_ref[...]
    for r0 in range(0, n, rows):
        acc = jnp.broadcast_to(bias, (rows, tc))
        for k in range(CONV_K):
            start = SUBLANES + r0 + k - halo
            acc = acc + pad_ref[start:start + rows, :] * w[k:k + 1, :]
        o_ref[0, r0:r0 + rows, :] = _silu(acc).astype(o_ref.dtype)


def _conv_silu(u, w, bias, tc=128):
    b, n, c = u.shape
    rows = min(512, n)
    return pl.pallas_call(
        functools.partial(_conv_kernel, rows=rows),
        grid=(b, c // tc),
        in_specs=[
            pl.BlockSpec((1, n, tc), lambda i, j: (i, 0, j)),
            pl.BlockSpec((CONV_K, tc), lambda i, j: (0, j)),
            pl.BlockSpec((1, tc), lambda i, j: (0, j)),
        ],
        out_specs=pl.BlockSpec((1, n, tc), lambda i, j: (i, 0, j)),
        out_shape=jax.ShapeDtypeStruct((b, n, c), BF16),
        scratch_shapes=[pltpu.VMEM((n + 2 * SUBLANES, tc), F32)],
        compiler_params=_params(("parallel", "parallel"), 40),
        name="conv_silu",
    )(u, w, bias.reshape(1, c))


def _attn_kernel(sink_ref, q_ref, kp_ref, ko_ref, kn_ref, vp_ref, vo_ref, vn_ref, kc_ref, vc_ref, o_ref):
    kv = pl.program_id(1)
    n = pl.program_id(2)
    nb = pl.num_programs(2)
    q = q_ref[0]
    qs = jnp.concatenate([q[:, g * HEAD_DIM:(g + 1) * HEAD_DIM] for g in range(GQA_GROUP)], axis=0)
    kb = jnp.concatenate([kp_ref[0], ko_ref[0], kn_ref[0], kc_ref[0]], axis=0)
    vb = jnp.concatenate([vp_ref[0], vo_ref[0], vn_ref[0], vc_ref[0]], axis=0)
    s = lax.dot_general(qs, kb, (((1,), (1,)), ((), ())), preferred_element_type=F32) * (HEAD_DIM ** -0.5)
    rows = GQA_GROUP * BLOCK
    qi = lax.broadcasted_iota(jnp.int32, (rows, BLOCK), 0) & (BLOCK - 1)
    kj = lax.broadcasted_iota(jnp.int32, (rows, BLOCK), 1)
    prev_ok = kj >= qi + jnp.where(n > 0, 0, BLOCK)
    next_ok = kj <= qi - jnp.where(n < nb - 1, 0, BLOCK)
    s = jnp.concatenate([
        jnp.where(prev_ok, s[:, 0:BLOCK], NEG_INF), s[:, BLOCK:2 * BLOCK],
        jnp.where(next_ok, s[:, 2 * BLOCK:3 * BLOCK], NEG_INF), s[:, 3 * BLOCK:]], axis=1)
    sink = jnp.concatenate(
        [jnp.full((BLOCK, 1), sink_ref[kv * GQA_GROUP + g], F32) for g in range(GQA_GROUP)], axis=0)
    m = jnp.maximum(jnp.max(s, axis=-1, keepdims=True), sink)
    p = jnp.exp(s - m)
    denom = jnp.sum(p, axis=-1, keepdims=True) + jnp.exp(sink - m)
    o = jnp.dot(p.astype(BF16), vb, preferred_element_type=F32) / denom
    o_ref[0] = jnp.concatenate([o[g * BLOCK:(g + 1) * BLOCK] for g in range(GQA_GROUP)], axis=1)


def _attention(a, kvc, sink, q_width, k_col0, v_col0):
    b, n, _ = a.shape
    lc = kvc.shape[1]
    nb = n // BLOCK
    kw = N_KV_HEADS * HEAD_DIM
    gw = GQA_GROUP * HEAD_DIM
    kb0, vb0 = k_col0 // HEAD_DIM, v_col0 // HEAD_DIM

    def kv_spec(col0, shift):
        return pl.BlockSpec(
            (1, BLOCK, HEAD_DIM),
            lambda i, k, j: (i, jnp.clip(j + shift, 0, nb - 1), col0 + k))

    return pl.pallas_call(
        _attn_kernel,
        grid=(b, N_KV_HEADS, nb),
        in_specs=[
            pl.BlockSpec(memory_space=pltpu.SMEM),
            pl.BlockSpec((1, BLOCK, gw), lambda i, k, j: (i, j, k)),
            kv_spec(kb0, -1), kv_spec(kb0, 0), kv_spec(kb0, 1),
            kv_spec(vb0, -1), kv_spec(vb0, 0), kv_spec(vb0, 1),
            pl.BlockSpec((1, lc, HEAD_DIM), lambda i, k, j: (i, 0, k)),
            pl.BlockSpec((1, lc, HEAD_DIM), lambda i, k, j: (i, 0, kw // HEAD_DIM + k)),
        ],
        out_specs=pl.BlockSpec((1, BLOCK, gw), lambda i, k, j: (i, j, k)),
        out_shape=jax.ShapeDtypeStruct((b, n, q_width), F32),
        compiler_params=_params(("parallel", "parallel", "arbitrary"), 32),
        name="window_attn",
    )(sink, a, a, a, a, a, a, a, kvc, kvc)


def _split_bf16(t, parts):
    out = []
    for _ in range(parts):
        hi = t.astype(BF16)
        out.append(hi)
        t = t - hi.astype(F32)
    return out


def _ssd_kernel(*refs, n_ctx_chunks, heads, inner):
    (xlf, blf, clf, dlf, xlb, blb, clb, dlb,
     xcf, bcf, ccf, dcf, xcb, bcb, ccb, dcb,
     alog_ref, bias_ref, e_ref, yf_ref, yb_ref,
     x_s, b_s, c_s, dt_s, h_s) = refs
    step = pl.program_id(1)
    hpg = heads // SSM_GROUPS
    gw = hpg * SSM_HEAD_DIM
    is_ctx = step < n_ctx_chunks

    @pl.when(step == 0)
    def _():
        h_s[...] = jnp.zeros_like(h_s)

    @pl.when(is_ctx)
    def _():
        for d, (xr, br, cr, dr) in enumerate(((xcf, bcf, ccf, dcf), (xcb, bcb, ccb, dcb))):
            x_s[d], b_s[d], c_s[d], dt_s[d] = xr[0], br[0], cr[0], dr[0]

    @pl.when(jnp.logical_not(is_ctx))
    def _():
        for d, (xr, br, cr, dr) in enumerate(((xlf, blf, clf, dlf), (xlb, blb, clb, dlb))):
            x_s[d], b_s[d], c_s[d], dt_s[d] = xr[0], br[0], cr[0], dr[0]

    lane = lax.broadcasted_iota(jnp.int32, (CHUNK, LANES), 1)
    row = lax.broadcasted_iota(jnp.int32, (CHUNK, CHUNK), 0)
    col = lax.broadcasted_iota(jnp.int32, (CHUNK, CHUNK), 1)
    is_fwd = lane < heads
    live = lane < 2 * heads

    dt_raw = jnp.where(is_fwd, dt_s[0], dt_s[1])
    dtv = jax.nn.softplus(dt_raw + bias_ref[...])
    a_row = jnp.where(live[0:1], -jnp.exp(alog_ref[...]), 0.0)
    da = dtv * a_row
    tri = jnp.where(row >= col, 1.0, 0.0).astype(BF16)
    acs = sum(jnp.dot(tri, part, preferred_element_type=F32) for part in _split_bf16(da, 3))
    tot = acs[CHUNK - 1:CHUNK, :]
    u = acs - jnp.where(is_fwd, 0.0, da)
    e_u = jnp.exp(u)
    e_r = jnp.exp(tot - u)
    rs = jnp.where(is_fwd, e_u, e_r)
    wst = jnp.where(is_fwd, e_r, e_u) * dtv
    packed = jnp.where(lane < 2 * heads, rs, pltpu.roll(wst, 2 * heads, 1))
    hi, lo = _split_bf16(packed, 2)
    packed2 = jnp.concatenate([hi, lo], axis=1)

    def expand(k):
        return jnp.dot(packed2, e_ref[:, k * inner:(k + 1) * inner], preferred_element_type=F32)

    u_t = u.T
    dt_t = dtv.T

    for d in range(2):
        rs_x = expand(d)
        w_x = expand(2 + d)
        dec = rs_x[CHUNK - 1:CHUNK, :] if d == 0 else rs_x[0:1, :]
        keep = (row >= col) if d == 0 else (col >= row)
        y_ref = yf_ref if d == 0 else yb_ref
        for g in range(SSM_GROUPS):
            bm = b_s[d, :, g * SSM_STATE:(g + 1) * SSM_STATE]
            cm = c_s[d, :, g * SSM_STATE:(g + 1) * SSM_STATE]
            xg = x_s[d, :, g * gw:(g + 1) * gw]
            cb = lax.dot_general(cm, bm, (((1,), (1,)), ((), ())), preferred_element_type=F32)
            h_old = h_s[d, :, g * gw:(g + 1) * gw]
            y_off = jnp.dot(cm, h_old.astype(BF16), preferred_element_type=F32) * rs_x[:, g * gw:(g + 1) * gw]
            pieces = []
            for r in range(0, hpg, 2):
                mats = []
                for hh in (g * hpg + r, g * hpg + r + 1):
                    ln = hh + d * heads
                    diff = u[:, ln:ln + 1] - u_t[ln:ln + 1, :]
                    if d == 1:
                        diff = -diff
                    decay = jnp.exp(jnp.where(keep, diff, -jnp.inf))
                    mats.append((cb * decay * dt_t[ln:ln + 1, :]).astype(BF16))
                xp = xg[:, r * SSM_HEAD_DIM:(r + 2) * SSM_HEAD_DIM]
                left = lane < SSM_HEAD_DIM
                zero = jnp.zeros_like(xp)
                rhs = jnp.concatenate([jnp.where(left, xp, zero), jnp.where(left, zero, xp)], axis=0)
                pieces.append(jnp.dot(jnp.concatenate(mats, axis=1), rhs, preferred_element_type=F32))
            y_g = jnp.concatenate(pieces, axis=1) + y_off

            @pl.when(jnp.logical_not(is_ctx))
            def _():
                y_ref[0, :, g * gw:(g + 1) * gw] = y_g

            xw = (xg.astype(F32) * w_x[:, g * gw:(g + 1) * gw]).astype(BF16)
            st = lax.dot_general(bm, xw, (((0,), (0,)), ((), ())), preferred_element_type=F32)
            h_s[d, :, g * gw:(g + 1) * gw] = h_old * dec[:, g * gw:(g + 1) * gw] + st


def _ssd(xbc_lat, dt_lat, xbc_ctx, dt_ctx, a_log, dt_bias, inner, heads):
    b, n, _ = xbc_lat.shape
    lc = xbc_ctx.shape[1]
    nlat, ncc = n // CHUNK, lc // CHUNK
    bc = SSM_GROUPS * SSM_STATE
    assert 4 * heads == LANES and inner % bc == 0
    b_blk, c_blk = inner // bc, inner // bc + 1

    def lat_f(s):
        return jnp.maximum(s - ncc, 0)

    def lat_b(s):
        return nlat - 1 - jnp.maximum(s - ncc, 0)

    def ctx_f(s):
        return jnp.minimum(s, ncc - 1)

    def ctx_b(s):
        return jnp.maximum(ncc - 1 - s, 0)

    def specs(chunk_of):
        return [
            pl.BlockSpec((1, CHUNK, inner), lambda i, s: (i, chunk_of(s), 0)),
            pl.BlockSpec((1, CHUNK, bc), lambda i, s: (i, chunk_of(s), b_blk)),
            pl.BlockSpec((1, CHUNK, bc), lambda i, s: (i, chunk_of(s), c_blk)),
        ]

    def dt_spec(chunk_of):
        return [pl.BlockSpec((1, CHUNK, LANES), lambda i, s: (i, chunk_of(s), 0))]

    in_specs = (specs(lat_f) + dt_spec(lat_f) + specs(lat_b) + dt_spec(lat_b)
                + specs(ctx_f) + dt_spec(ctx_f) + specs(ctx_b) + dt_spec(ctx_b))
    const = lambda shape: pl.BlockSpec(shape, lambda i, s: (0, 0))
    in_specs += [const((1, LANES)), const((1, LANES)), const((2 * LANES, 4 * inner))]

    pad = jnp.zeros((LANES - 2 * heads,), F32)
    alog_row = jnp.concatenate([a_log.reshape(-1).astype(F32), pad]).reshape(1, LANES)
    bias_row = jnp.concatenate([dt_bias.reshape(-1).astype(F32), pad]).reshape(1, LANES)
    cidx = jnp.arange(4 * inner)
    src = heads * (cidx // inner) + (cidx % inner) // SSM_HEAD_DIM
    expand = (jnp.arange(LANES)[:, None] == src[None, :]).astype(BF16)
    expand = jnp.concatenate([expand, expand], axis=0)

    lat = [xbc_lat, xbc_lat, xbc_lat, dt_lat]
    ctx = [xbc_ctx, xbc_ctx, xbc_ctx, dt_ctx]
    y_shape = jax.ShapeDtypeStruct((b, n, inner), F32)
    return pl.pallas_call(
        functools.partial(_ssd_kernel, n_ctx_chunks=ncc, heads=heads, inner=inner),
        grid=(b, ncc + nlat),
        in_specs=in_specs,
        out_specs=[
            pl.BlockSpec((1, CHUNK, inner), lambda i, s: (i, lat_f(s), 0)),
            pl.BlockSpec((1, CHUNK, inner), lambda i, s: (i, lat_b(s), 0)),
        ],
        out_shape=[y_shape, y_shape],
        scratch_shapes=[
            pltpu.VMEM((2, CHUNK, inner), BF16),
            pltpu.VMEM((2, CHUNK, bc), BF16),
            pltpu.VMEM((2, CHUNK, bc), BF16),
            pltpu.VMEM((2, CHUNK, LANES), F32),
            pltpu.VMEM((2, SSM_STATE, inner), F32),
        ],
        compiler_params=_params(("parallel", "arbitrary"), 48),
        name="ssd_scan",
    )(*lat, *lat, *ctx, *ctx, alog_row, bias_row, expand)


def _merge_kernel(attn_ref, yf_ref, yb_ref, xs_ref, z_ref, d_ref, an_ref, sn_ref, o_ref):
    aw = attn_ref.shape[2]
    a = attn_ref[0]
    o_ref[0, :, 0:aw] = (a * _rms_scale(a) * an_ref[...]).astype(BF16)
    y = yf_ref[0] + yb_ref[0] + d_ref[...] * xs_ref[0].astype(F32)
    t = y * _silu(z_ref[0].astype(F32))
    o_ref[0, :, aw:] = (t * _rms_scale(t) * sn_ref[...]).astype(BF16)


def _merge(attn, y_f, y_b, xbc, a, z_col0, d_exp, attn_norm, ssm_norm, tr=256):
    b, n, aw = attn.shape
    inner = y_f.shape[2]
    row = lambda w, cb=0: pl.BlockSpec((1, tr, w), lambda i, r: (i, r, cb))
    par = lambda w: pl.BlockSpec((1, w), lambda i, r: (0, 0))
    return pl.pallas_call(
        _merge_kernel,
        grid=(b, n // tr),
        in_specs=[row(aw), row(inner), row(inner), row(inner), row(inner, z_col0 // inner),
                  par(inner), par(aw), par(inner)],
        out_specs=row(aw + inner),
        out_shape=jax.ShapeDtypeStruct((b, n, aw + inner), BF16),
        compiler_params=_params(("parallel", "parallel"), 40),
        name="merge",
    )(attn, y_f, y_b, xbc, a, d_exp.reshape(1, inner), attn_norm.reshape(1, aw), ssm_norm.reshape(1, inner))


def kernel(x, c, ctx, c_ctx, w_mod, b_mod, norm_pre, norm_post, w_ffn_gate, w_ffn_up, w_ffn_down,
           w_in, attn_sink, attn_norm, conv_w, conv_b, a_log, dt_bias, d_skip, ssm_norm, w_out):
    bsz, n, d = x.shape
    lc = ctx.shape[1]
    depth = w_mod.shape[0]
    attn_w = N_HEADS * HEAD_DIM
    inner = d // 2
    heads = inner // SSM_HEAD_DIM
    kv_w = N_KV_HEADS * HEAD_DIM
    bc_w = SSM_GROUPS * SSM_STATE
    xbc_w = inner + 2 * bc_w
    ctx_col0 = attn_w + inner
    xbc_col0 = ctx_col0 + 2 * kv_w
    dt_col0 = xbc_col0 + xbc_w
    rope_tabs = _rope_tables(n)
    tile = 512

    for layer in range(depth):
        last = layer == depth - 1
        cv = jnp.zeros((SUBLANES, d), F32).at[:bsz].set(c).at[bsz].set(c_ctx)
        mod = _adaln(cv, w_mod[layer], b_mod[layer]).reshape(SUBLANES, N_MOD, d)
        mod_x = mod[:bsz]
        mod_c = jnp.broadcast_to(mod[bsz:bsz + 1], (bsz, N_MOD, d))

        def ffn_weights(slot):
            return (w_ffn_gate[layer, slot].astype(BF16), w_ffn_up[layer, slot].astype(BF16),
                    w_ffn_down[layer, slot].astype(BF16))

        w0 = ffn_weights(0)
        h = _prenorm(x, norm_pre[layer, 0], mod_x[:, 0], mod_x[:, 1])
        y = _ffn(h.reshape(bsz * n, d), *w0)
        x, hx = _resid(x, y, norm_post[layer, 0], mod_x[:, 2], 0.5,
                       nxt=(norm_pre[layer, 1], mod_x[:, 3], mod_x[:, 4]))
        hcx = _prenorm(ctx, norm_pre[layer, 0], mod_c[:, 0], mod_c[:, 1])
        yc = _ffn(hcx.reshape(bsz * lc, d), *w0)
        ctx, hc = _resid(ctx, yc, norm_post[layer, 0], mod_c[:, 2], 0.5,
                         nxt=(norm_pre[layer, 1], mod_c[:, 3], mod_c[:, 4]))

        w_in_l = w_in[layer]
        w_a = w_in_l[:, :xbc_col0].astype(BF16)
        w_xbc = w_in_l[:, xbc_col0:dt_col0].astype(BF16)
        w_dt = jnp.pad(w_in_l[:, dt_col0:], ((0, 0), (0, LANES - 2 * heads))).astype(BF16)
        hx2, hc2 = hx.reshape(bsz * n, d), hc.reshape(bsz * lc, d)
        q_tiles = (0, attn_w // tile)
        k_tiles = (ctx_col0 // tile, (ctx_col0 + kv_w) // tile)
        a_lat = _proj(hx2, w_a, BF16, rope=(rope_tabs, n, (q_tiles, k_tiles))).reshape(bsz, n, xbc_col0)
        xbc_lat = _proj(hx2, w_xbc, F32).reshape(bsz, n, xbc_w)
        dt_lat = _proj(hx2, w_dt, F32).reshape(bsz, n, LANES)
        kv_ctx = _proj(hc2, w_a, BF16, col0=ctx_col0, n_cols=2 * kv_w).reshape(bsz, lc, 2 * kv_w)
        xbc_ctx = _proj(hc2, w_xbc, F32).reshape(bsz, lc, xbc_w)
        dt_ctx = _proj(hc2, w_dt, F32).reshape(bsz, lc, LANES)

        attn = _attention(a_lat, kv_ctx, attn_sink[layer].astype(F32), attn_w, ctx_col0, ctx_col0 + kv_w)
        xbc_lat = _conv_silu(xbc_lat, conv_w[layer], conv_b[layer])
        xbc_ctx = _conv_silu(xbc_ctx, conv_w[layer], conv_b[layer])
        y_f, y_b = _ssd(xbc_lat, dt_lat, xbc_ctx, dt_ctx, a_log[layer], dt_bias[layer], inner, heads)
        d_exp = jnp.repeat(d_skip[layer].astype(F32), SSM_HEAD_DIM)
        merged = _merge(attn, y_f, y_b, xbc_lat, a_lat, attn_w, d_exp, attn_norm[layer], ssm_norm[layer])
        y = _proj(merged.reshape(bsz * n, attn_w + inner), w_out[layer].astype(BF16), F32)
        x, h = _resid(x, y, norm_post[layer, 1], mod_x[:, 5], 1.0,
                      nxt=(norm_pre[layer, 2], mod_x[:, 6], mod_x[:, 7]))

        w1 = ffn_weights(1)
        y = _ffn(h.reshape(bsz * n, d), *w1)
        x = _resid(x, y, norm_post[layer, 2], mod_x[:, 8], 0.5)
        if not last:
            raise NotImplementedError("context output path is only needed for depth > 1")
    return x
```

```python
import functools

import jax
import jax.numpy as jnp
from jax import lax
from jax.experimental import pallas as pl
from jax.experimental.pallas import tpu as pltpu

GRID_W = 64
HEAD_DIM = 128
N_HEADS = 16
N_KV_HEADS = 4
GQA_GROUP = N_HEADS // N_KV_HEADS
BLOCK = 128
ROPE_BASE = 10000.0
ROPE_PAIRS = HEAD_DIM // 4
SSM_HEAD_DIM = 64
SSM_GROUPS = 4
SSM_STATE = 128
CONV_K = 5
CHUNK = 128
N_MOD = 9
EPS = 1e-6
NEG_INF = -1e30

LANES = 128
SUBLANES = 8
MIB = 1024 * 1024

F32 = jnp.float32
BF16 = jnp.bfloat16


def _params(semantics, vmem_mib):
    return pltpu.CompilerParams(dimension_semantics=semantics, vmem_limit_bytes=vmem_mib * MIB)


def _silu(t):
    return t * jax.nn.sigmoid(t)


def _rms_scale(t):
    return lax.rsqrt(jnp.mean(t * t, axis=-1, keepdims=True) + EPS)


def _mod_kernel(c_ref, w_ref, b_ref, o_ref):
    s = _silu(c_ref[...]).astype(BF16)
    o_ref[...] = jnp.dot(s, w_ref[...].astype(BF16), preferred_element_type=F32) + b_ref[...]


def _adaln(cv, w_mod, b_mod):
    rows, d = cv.shape
    n = w_mod.shape[1]
    tn = 512
    return pl.pallas_call(
        _mod_kernel,
        grid=(n // tn,),
        in_specs=[
            pl.BlockSpec((rows, d), lambda j: (0, 0)),
            pl.BlockSpec((d, tn), lambda j: (0, j)),
            pl.BlockSpec((1, tn), lambda j: (0, j)),
        ],
        out_specs=pl.BlockSpec((rows, tn), lambda j: (0, j)),
        out_shape=jax.ShapeDtypeStruct((rows, n), F32),
        compiler_params=_params(("parallel",), 40),
        name="adaln",
    )(cv, w_mod, b_mod.reshape(1, n))


def _prenorm_kernel(s_ref, g_ref, sh_ref, sc_ref, h_ref):
    s = s_ref[0]
    h = s * _rms_scale(s) * g_ref[...]
    h_ref[0] = (h * (1.0 + sc_ref[0]) + sh_ref[0]).astype(BF16)


def _prenorm(s, g_pre, shift, scale, tr=256):
    b, n, d = s.shape
    row = pl.BlockSpec((1, tr, d), lambda i, r: (i, r, 0))
    vec = pl.BlockSpec((1, 1, d), lambda i, r: (i, 0, 0))
    return pl.pallas_call(
        _prenorm_kernel,
        grid=(b, n // tr),
        in_specs=[row, pl.BlockSpec((1, d), lambda i, r: (0, 0)), vec, vec],
        out_specs=row,
        out_shape=jax.ShapeDtypeStruct((b, n, d), BF16),
        compiler_params=_params(("parallel", "parallel"), 32),
        name="prenorm",
    )(s, g_pre.reshape(1, d), shift.reshape(b, 1, d), scale.reshape(b, 1, d))


def _resid_kernel(*refs, coef, with_h):
    if with_h:
        s_ref, y_ref, gp_ref, gate_ref, gn_ref, sh_ref, sc_ref, o_ref, h_ref = refs
    else:
        s_ref, y_ref, gp_ref, gate_ref, o_ref = refs
    y = y_ref[0].astype(F32)
    s_new =s_ref[0] + (coef * gate_ref[0]) * (y * _rms_scale(y) * gp_ref[...])
    o_ref[0] = s_new
    if with_h:
        h = s_new * _rms_scale(s_new) * gn_ref[...]
        h_ref[0] = (h * (1.0 + sc_ref[0]) + sh_ref[0]).astype(BF16)


def _resid(s, y, g_post, gate, coef, nxt=None, tr=256):
    b, n, d = s.shape
    row = pl.BlockSpec((1, tr, d), lambda i, r: (i, r, 0))
    vec = pl.BlockSpec((1, 1, d), lambda i, r: (i, 0, 0))
    par = pl.BlockSpec((1, d), lambda i, r: (0, 0))
    args = [s, y.reshape(b, n, d), g_post.reshape(1, d), gate.reshape(b, 1, d)]
    in_specs = [row, row, par, vec]
    out_shape = [jax.ShapeDtypeStruct((b, n, d), F32)]
    out_specs = [row]
    if nxt is not None:
        g_next, shift, scale = nxt
        args += [g_next.reshape(1, d), shift.reshape(b, 1, d), scale.reshape(b, 1, d)]
        in_specs += [par, vec, vec]
        out_shape.append(jax.ShapeDtypeStruct((b, n, d), BF16))
        out_specs.append(row)
    out = pl.pallas_call(
        functools.partial(_resid_kernel, coef=coef, with_h=nxt is not None),
        grid=(b, n // tr),
        in_specs=in_specs,
        out_specs=out_specs,
        out_shape=out_shape,
        compiler_params=_params(("parallel", "parallel"), 48),
        name="resid",
    )(*args)
    return out if nxt is not None else out[0]


def _ffn_kernel(h_ref, wg_ref, wu_ref, wd_ref, y_ref, acc_ref, *, n_chunk):
    j = pl.program_id(1)

    @pl.when(j == 0)
    def _():
        acc_ref[...] = jnp.zeros_like(acc_ref)

    h = h_ref[...]
    g = jnp.dot(h, wg_ref[...], preferred_element_type=F32)
    u = jnp.dot(h, wu_ref[...], preferred_element_type=F32)
    a = (_silu(g) * u).astype(BF16)
    d = acc_ref.shape[1]
    for n0 in range(0, d, n_chunk):
        acc_ref[:, n0:n0 + n_chunk] += jnp.dot(a, wd_ref[:, n0:n0 + n_chunk], preferred_element_type=F32)

    @pl.when(j == pl.num_programs(1) - 1)
    def _():
        y_ref[...] = acc_ref[...].astype(y_ref.dtype)


def _ffn(h, wg, wu, wd, layer, slot, tm=1024, tf=256):
    r, d = h.shape
    f = wg.shape[3]
    tm = min(tm, r)
    once = pl.Buffered(1)
    return pl.pallas_call(
        functools.partial(_ffn_kernel, n_chunk=1024),
        grid=(r // tm, f // tf),
        in_specs=[
            pl.BlockSpec((tm, d), lambda i, j: (i, 0), pipeline_mode=once),
            pl.BlockSpec((None, None, d, tf), lambda i, j: (layer, slot, 0, j)),
            pl.BlockSpec((None, None, d, tf), lambda i, j: (layer, slot, 0, j)),
            pl.BlockSpec((None, None, tf, d), lambda i, j: (layer, slot, j, 0)),
        ],
        out_specs=pl.BlockSpec((tm, d), lambda i, j: (i, 0), pipeline_mode=once),
        out_shape=jax.ShapeDtypeStruct((r, d), BF16),
        scratch_shapes=[pltpu.VMEM((tm, d), F32)],
        compiler_params=_params(("parallel", "arbitrary"), 56),
        name="ffn",
    )(h, wg, wu, wd)


def _proj_kernel(*refs, q_tiles, k_tiles):
    rope = q_tiles is not None
    if rope:
        h_ref, w_ref, cos_ref, sa_ref, sb_ref, o_ref = refs
    else:
        h_ref, w_ref, o_ref = refs
    res = jnp.dot(h_ref[...], w_ref[...], preferred_element_type=F32)
    if not rope:
        o_ref[...] = res.astype(o_ref.dtype)
        return
    j = pl.program_id(1)
    is_q = (j >= q_tiles[0]) & (j < q_tiles[1])
    is_rope = is_q | ((j >= k_tiles[0]) & (j < k_tiles[1]))

    @pl.when(is_rope)
    def _():
        scale = jnp.where(is_q, HEAD_DIM ** -0.5, 1.0).astype(F32)
        cos, sa, sb = cos_ref[...] * scale, sa_ref[...] * scale, sb_ref[...] * scale
        for k in range(res.shape[1] // HEAD_DIM):
            t = res[:, k * HEAD_DIM:(k + 1) * HEAD_DIM]
            o = t * cos + pltpu.roll(t, ROPE_PAIRS, 1) * sa + pltpu.roll(t, HEAD_DIM - ROPE_PAIRS, 1) * sb
            o_ref[:, k * HEAD_DIM:(k + 1) * HEAD_DIM] = o.astype(o_ref.dtype)

    @pl.when(jnp.logical_not(is_rope))
    def _():
        o_ref[...] = res.astype(o_ref.dtype)


def _proj(h, w, layer, out_dtype, *, col0=0, n_cols=None, tm=1024, tn=512, rope=None):
    r, kdim = h.shape
    n_cols = w.shape[2] - col0 if n_cols is None else n_cols
    tm = min(tm, r)
    tn = min(tn, n_cols)
    jo = col0 // tn
    in_specs = [
        pl.BlockSpec((tm, kdim), lambda i, j: (i, 0)),
        pl.BlockSpec((None, kdim, tn), lambda i, j: (layer, 0, j + jo)),
    ]
    args = [h, w]
    q_tiles = k_tiles = None
    if rope is not None:
        tables, seq_len, q_tiles, k_tiles = rope
        per_seq = seq_len // tm
        tab = pl.BlockSpec((tm, HEAD_DIM), lambda i, j: (i % per_seq, 0))
        in_specs += [tab, tab, tab]
        args += list(tables)
    return pl.pallas_call(
        functools.partial(_proj_kernel, q_tiles=q_tiles, k_tiles=k_tiles),
        grid=(r // tm, n_cols // tn),
        in_specs=in_specs,
        out_specs=pl.BlockSpec((tm, tn), lambda i, j: (i, j)),
        out_shape=jax.ShapeDtypeStruct((r, n_cols), out_dtype),
        compiler_params=_params(("parallel", "arbitrary"), 40),
        name="proj",
    )(*args)


def _rope_tables(n):
    row = (jnp.arange(n) // GRID_W).astype(F32)
    col = (jnp.arange(n) % GRID_W).astype(F32)
    inv = ROPE_BASE ** (-jnp.arange(ROPE_PAIRS, dtype=F32) / ROPE_PAIRS)
    ar = row[:, None] * inv
    ac = col[:, None] * inv
    ang = jnp.concatenate([ar, ar, ac, ac], axis=-1)
    cos, sin = jnp.cos(ang), jnp.sin(ang)
    upper = (jnp.arange(HEAD_DIM) % (2 * ROPE_PAIRS)) >= ROPE_PAIRS
    return cos, jnp.where(upper, sin, 0.0), jnp.where(upper, 0.0, -sin)


def _conv_kernel(x_ref, w_ref, b_ref, o_ref, pad_ref, *, rows):
    n = x_ref.shape[1]
    tc = x_ref.shape[2]
    halo = CONV_K // 2
    zeros = jnp.zeros((SUBLANES, tc), F32)
    pad_ref[0:SUBLANES, :] = zeros
    pad_ref[n + SUBLANES:n + 2 * SUBLANES, :] = zeros
    pad_ref[SUBLANES:n + SUBLANES, :] = x_ref[0].astype(F32)
    w = w_ref[...]
    bias = b_ref[...]
    for r0 in range(0, n, rows):
        acc = jnp.broadcast_to(bias, (rows, tc))
        for k in range(CONV_K):
            start = SUBLANES + r0 + k - halo
            acc = acc + pad_ref[start:start + rows, :] * w[k:k + 1, :]
        o_ref[0, r0:r0 + rows, :] = _silu(acc).astype(o_ref.dtype)


def _conv_silu(u, w, bias, tc=128):
    b, n, c = u.shape
    rows = min(512, n)
    return pl.pallas_call(
        functools.partial(_conv_kernel, rows=rows),
        grid=(b, c // tc),
        in_specs=[
            pl.BlockSpec((1, n, tc), lambda i, j: (i, 0, j)),
            pl.BlockSpec((CONV_K, tc), lambda i, j: (0, j)),
            pl.BlockSpec((1, tc), lambda i, j: (0, j)),
        ],
        out_specs=pl.BlockSpec((1, n, tc), lambda i, j: (i, 0, j)),
        out_shape=jax.ShapeDtypeStruct((b, n, c), BF16),
        scratch_shapes=[pltpu.VMEM((n + 2 * SUBLANES, tc), F32)],
        compiler_params=_params(("parallel", "parallel"), 40),
        name="conv_silu",
    )(u, w, bias.reshape(1, c))


def _attn_kernel(sink_ref, q_ref, kp_ref, ko_ref, kn_ref, vp_ref, vo_ref, vn_ref, kc_ref, vc_ref, o_ref):
    n = pl.program_id(1)
    nb = pl.num_programs(1)
    rows = GQA_GROUP * BLOCK
    qi = lax.broadcasted_iota(jnp.int32, (rows, BLOCK), 0) & (BLOCK - 1)
    kj = lax.broadcasted_iota(jnp.int32, (rows, BLOCK), 1)
    prev_ok = kj >= qi + jnp.where(n > 0, 0, BLOCK)
    next_ok = kj <= qi - jnp.where(n < nb - 1, 0, BLOCK)
    for kv in range(N_KV_HEADS):
        heads = range(kv * GQA_GROUP, (kv + 1) * GQA_GROUP)
        cols = slice(kv * HEAD_DIM, (kv + 1) * HEAD_DIM)
        qs = jnp.concatenate([q_ref[0, :, h * HEAD_DIM:(h + 1) * HEAD_DIM] for h in heads], axis=0)
        kb = jnp.concatenate([kp_ref[0, :, cols], ko_ref[0, :, cols], kn_ref[0, :, cols], kc_ref[0, :, cols]], axis=0)
        vb = jnp.concatenate([vp_ref[0, :, cols], vo_ref[0, :, cols], vn_ref[0, :, cols], vc_ref[0, :, cols]], axis=0)
        s = lax.dot_general(qs, kb, (((1,), (1,)), ((), ())), preferred_element_type=F32)
        s = jnp.concatenate([
            jnp.where(prev_ok, s[:, 0:BLOCK], NEG_INF), s[:, BLOCK:2 * BLOCK],
            jnp.where(next_ok, s[:, 2 * BLOCK:3 * BLOCK], NEG_INF), s[:, 3 * BLOCK:]], axis=1)
        sink = jnp.concatenate([jnp.full((BLOCK, 1), sink_ref[h], F32) for h in heads], axis=0)
        m = jnp.maximum(jnp.max(s, axis=-1, keepdims=True), sink)
        p = jnp.exp(s - m)
        denom = jnp.sum(p, axis=-1, keepdims=True) + jnp.exp(sink - m)
        o = jnp.dot(p.astype(BF16), vb, preferred_element_type=F32) / denom
        for g, h in enumerate(heads):
            o_ref[0, :, h * HEAD_DIM:(h + 1) * HEAD_DIM] = o[g * BLOCK:(g + 1) * BLOCK].astype(o_ref.dtype)


def _attention(a, kvc, sink, q_width, k_col0, v_col0):
    b, n, _ = a.shape
    lc = kvc.shape[1]
    nb = n // BLOCK
    kw = N_KV_HEADS * HEAD_DIM
    kb0, vb0 = k_col0 // kw, v_col0 // kw

    def kv_spec(col_blk, shift):
        return pl.BlockSpec((1, BLOCK, kw), lambda i, j: (i, jnp.clip(j + shift, 0, nb - 1), col_blk))

    return pl.pallas_call(
        _attn_kernel,
        grid=(b, nb),
        in_specs=[
            pl.BlockSpec(memory_space=pltpu.SMEM),
            pl.BlockSpec((1, BLOCK, q_width), lambda i, j: (i, j, 0)),
            kv_spec(kb0, -1), kv_spec(kb0, 0), kv_spec(kb0, 1),
            kv_spec(vb0, -1), kv_spec(vb0, 0), kv_spec(vb0, 1),
            pl.BlockSpec((1, lc, kw), lambda i, j: (i, 0, 0)),
            pl.BlockSpec((1, lc, kw), lambda i, j: (i, 0, 1)),
        ],
        out_specs=pl.BlockSpec((1, BLOCK, q_width), lambda i, j: (i, j, 0)),
        out_shape=jax.ShapeDtypeStruct((b, n, q_width), BF16),
        compiler_params=_params(("parallel", "arbitrary"), 32),
        name="window_attn",
    )(sink, a, a, a, a, a, a, a, kvc, kvc)


def _split_bf16(t, parts):
    out = []
    for _ in range(parts):
        hi = t.astype(BF16)
        out.append(hi)
        t = t - hi.astype(F32)
    return out


def _ssd_kernel(*refs, n_ctx_chunks, heads, inner):
    (xlf, blf, clf, dlf, xlb, blb, clb, dlb,
     xcf, bcf, ccf, dcf, xcb, bcb, ccb, dcb,
     alog_ref, bias_ref, e_ref, yf_ref, yb_ref,
     x_s, b_s, c_s, dt_s, h_s) = refs
    step = pl.program_id(1)
    hpg = heads // SSM_GROUPS
    gw = hpg * SSM_HEAD_DIM
    is_ctx = step < n_ctx_chunks

    @pl.when(step == 0)
    def _():
        h_s[...] = jnp.zeros_like(h_s)

    @pl.when(is_ctx)
    def _():
        for d, (xr, br, cr, dr) in enumerate(((xcf, bcf, ccf, dcf), (xcb, bcb, ccb, dcb))):
            x_s[d], b_s[d], c_s[d], dt_s[d] = xr[0], br[0], cr[0], dr[0]

    @pl.when(jnp.logical_not(is_ctx))
    def _():
        for d, (xr, br, cr, dr) in enumerate(((xlf, blf, clf, dlf), (xlb, blb, clb, dlb))):
            x_s[d], b_s[d], c_s[d], dt_s[d] = xr[0], br[0], cr[0], dr[0]

    lane = lax.broadcasted_iota(jnp.int32, (CHUNK, LANES), 1)
    row = lax.broadcasted_iota(jnp.int32, (CHUNK, CHUNK), 0)
    col = lax.broadcasted_iota(jnp.int32, (CHUNK, CHUNK), 1)
    is_fwd = lane < heads
    live = lane < 2 * heads

    dt_raw = jnp.where(is_fwd, dt_s[0], dt_s[1])
    dtv = jax.nn.softplus(dt_raw + bias_ref[...])
    a_row = jnp.where(live[0:1], -jnp.exp(alog_ref[...]), 0.0)
    da = dtv * a_row
    tri = jnp.where(row >= col, 1.0, 0.0).astype(BF16)
    acs = sum(jnp.dot(tri, part, preferred_element_type=F32) for part in _split_bf16(da, 3))
    tot = acs[CHUNK - 1:CHUNK, :]
    u = acs - jnp.where(is_fwd, 0.0, da)
    e_u = jnp.exp(u)
    e_r = jnp.exp(tot - u)
    rs = jnp.where(is_fwd, e_u, e_r)
    wst = jnp.where(is_fwd, e_r, e_u) * dtv
    packed = jnp.where(lane < 2 * heads, rs, pltpu.roll(wst, 2 * heads, 1))
    hi, lo = _split_bf16(packed, 2)
    packed2 = jnp.concatenate([hi, lo], axis=1)

    def expand(k):
        return jnp.dot(packed2, e_ref[:, k * inner:(k + 1) * inner], preferred_element_type=F32)

    u_t = u.T
    dt_t = dtv.T

    for d in range(2):
        rs_x = expand(d)
        w_x = expand(2 + d)
        dec = rs_x[CHUNK - 1:CHUNK, :] if d == 0 else rs_x[0:1, :]
        keep = (row >= col) if d == 0 else (col >= row)
        y_ref = yf_ref if d == 0 else yb_ref
        for g in range(SSM_GROUPS):
            bm = b_s[d, :, g * SSM_STATE:(g + 1) * SSM_STATE]
            cm = c_s[d, :, g * SSM_STATE:(g + 1) * SSM_STATE]
            xg = x_s[d, :, g * gw:(g + 1) * gw]
            cb = lax.dot_general(cm, bm, (((1,), (1,)), ((), ())), preferred_element_type=F32)
            h_old = h_s[d, :, g * gw:(g + 1) * gw]
            y_off = jnp.dot(cm, h_old.astype(BF16), preferred_element_type=F32) * rs_x[:, g * gw:(g + 1) * gw]
            pieces = []
            for r in range(0, hpg, 2):
                mats = []
                for hh in (g * hpg + r, g * hpg + r + 1):
                    ln = hh + d * heads
                    diff = u[:, ln:ln + 1] - u_t[ln:ln + 1, :]
                    if d == 1:
                        diff = -diff
                    decay = jnp.exp(jnp.where(keep, diff, -jnp.inf))
                    mats.append((cb * decay * dt_t[ln:ln + 1, :]).astype(BF16))
                xp = xg[:, r * SSM_HEAD_DIM:(r + 2) * SSM_HEAD_DIM]
                left = lane < SSM_HEAD_DIM
                zero = jnp.zeros_like(xp)
                rhs = jnp.concatenate([jnp.where(left, xp, zero), jnp.where(left, zero, xp)], axis=0)
                pieces.append(jnp.dot(jnp.concatenate(mats, axis=1), rhs, preferred_element_type=F32))
            y_g = jnp.concatenate(pieces, axis=1) + y_off

            @pl.when(jnp.logical_not(is_ctx))
            def _():
                y_ref[0, :, g * gw:(g + 1) * gw] = y_g.astype(y_ref.dtype)

            xw = (xg.astype(F32) * w_x[:, g * gw:(g + 1) * gw]).astype(BF16)
            st = lax.dot_general(bm, xw, (((0,), (0,)), ((), ())), preferred_element_type=F32)
            h_s[d, :, g * gw:(g + 1) * gw] = h_old * dec[:, g * gw:(g + 1) * gw] + st


def _ssd(xbc_lat, dt_lat, xbc_ctx, dt_ctx, a_log, dt_bias, inner, heads):
    b, n, _ = xbc_lat.shape
    lc = xbc_ctx.shape[1]
    nlat, ncc = n // CHUNK, lc // CHUNK
    bc = SSM_GROUPS * SSM_STATE
    assert 4 * heads == LANES and inner % bc == 0
    b_blk, c_blk = inner // bc, inner // bc + 1

    def lat_f(s):
        return jnp.maximum(s - ncc, 0)

    def lat_b(s):
        return nlat - 1 - jnp.maximum(s - ncc, 0)

    def ctx_f(s):
        return jnp.minimum(s, ncc - 1)

    def ctx_b(s):
        return jnp.maximum(ncc - 1 - s, 0)

    def specs(chunk_of):
        return [
            pl.BlockSpec((1, CHUNK, inner), lambda i, s: (i, chunk_of(s), 0)),
            pl.BlockSpec((1, CHUNK, bc), lambda i, s: (i, chunk_of(s), b_blk)),
            pl.BlockSpec((1, CHUNK, bc), lambda i, s: (i, chunk_of(s), c_blk)),
        ]

    def dt_spec(chunk_of):
        return [pl.BlockSpec((1, CHUNK, LANES), lambda i, s: (i, chunk_of(s), 0))]

    in_specs = (specs(lat_f) + dt_spec(lat_f) + specs(lat_b) + dt_spec(lat_b)
                + specs(ctx_f) + dt_spec(ctx_f) + specs(ctx_b) + dt_spec(ctx_b))
    const = lambda shape: pl.BlockSpec(shape, lambda i, s: (0, 0))
    in_specs += [const((1, LANES)), const((1, LANES)), const((2 * LANES, 4 * inner))]

    pad = jnp.zeros((LANES - 2 * heads,), F32)
    alog_row = jnp.concatenate([a_log.reshape(-1).astype(F32), pad]).reshape(1, LANES)
    bias_row = jnp.concatenate([dt_bias.reshape(-1).astype(F32), pad]).reshape(1, LANES)
    cidx = jnp.arange(4 * inner)
    src = heads * (cidx // inner) + (cidx % inner) // SSM_HEAD_DIM
    expand = (jnp.arange(LANES)[:, None] == src[None, :]).astype(BF16)
    expand = jnp.concatenate([expand, expand], axis=0)

    lat = [xbc_lat, xbc_lat, xbc_lat, dt_lat]
    ctx = [xbc_ctx, xbc_ctx, xbc_ctx, dt_ctx]
    y_shape = jax.ShapeDtypeStruct((b, n, inner), BF16)
    return pl.pallas_call(
        functools.partial(_ssd_kernel, n_ctx_chunks=ncc, heads=heads, inner=inner),
        grid=(b, ncc + nlat),
        in_specs=in_specs,
        out_specs=[
            pl.BlockSpec((1, CHUNK, inner), lambda i, s: (i, lat_f(s), 0)),
            pl.BlockSpec((1, CHUNK, inner), lambda i, s: (i, lat_b(s), 0)),
        ],
        out_shape=[y_shape, y_shape],
        scratch_shapes=[
            pltpu.VMEM((2, CHUNK, inner), BF16),
            pltpu.VMEM((2, CHUNK, bc), BF16),
            pltpu.VMEM((2, CHUNK, bc), BF16),
            pltpu.VMEM((2, CHUNK, LANES), F32),
            pltpu.VMEM((2, SSM_STATE, inner), F32),
        ],
        compiler_params=_params(("parallel", "arbitrary"), 48),
        name="ssd_scan",
    )(*lat, *lat, *ctx, *ctx, alog_row, bias_row, expand)


def _merge_kernel(attn_ref, yf_ref, yb_ref, xs_ref, z_ref, d_ref, an_ref, sn_ref, o_ref):
    aw = attn_ref.shape[2]
    a = attn_ref[0].astype(F32)
    o_ref[0, :, 0:aw] = (a * _rms_scale(a) * an_ref[...]).astype(BF16)
    y = yf_ref[0].astype(F32) + yb_ref[0].astype(F32) + d_ref[...] * xs_ref[0].astype(F32)
    t = y * _silu(z_ref[0].astype(F32))
    o_ref[0, :, aw:] = (t * _rms_scale(t) * sn_ref[...]).astype(BF16)


def _merge(attn, y_f, y_b, xbc, a, z_col0, d_exp, attn_norm, ssm_norm, tr=256):
    b, n, aw = attn.shape
    inner = y_f.shape[2]
    row = lambda w, cb=0: pl.BlockSpec((1, tr, w), lambda i, r: (i, r, cb))
    par = lambda w: pl.BlockSpec((1, w), lambda i, r: (0, 0))
    return pl.pallas_call(
        _merge_kernel,
        grid=(b, n // tr),
        in_specs=[row(aw), row(inner), row(inner), row(inner), row(inner, z_col0 // inner),
                  par(inner), par(aw), par(inner)],
        out_specs=row(aw + inner),
        out_shape=jax.ShapeDtypeStruct((b, n, aw + inner), BF16),
        compiler_params=_params(("parallel", "parallel"), 40),
        name="merge",
    )(attn, y_f, y_b, xbc, a, d_exp.reshape(1, inner), attn_norm.reshape(1, aw), ssm_norm.reshape(1, inner))


def kernel(x, c, ctx, c_ctx, w_mod, b_mod, norm_pre, norm_post, w_ffn_gate, w_ffn_up, w_ffn_down,
           w_in, attn_sink, attn_norm, conv_w, conv_b, a_log, dt_bias, d_skip, ssm_norm, w_out):
    bsz, n, d = x.shape
    lc = ctx.shape[1]
    depth = w_mod.shape[0]
    attn_w = N_HEADS * HEAD_DIM
    inner = d // 2
    heads = inner // SSM_HEAD_DIM
    kv_w = N_KV_HEADS * HEAD_DIM
    bc_w = SSM_GROUPS * SSM_STATE
    xbc_w = inner + 2 * bc_w
    ctx_col0 = attn_w + inner
    xbc_col0 = ctx_col0 + 2 * kv_w
    dt_col0 = xbc_col0 + xbc_w
    rope_tabs = _rope_tables(n)
    tile = 512
    wg_all, wu_all, wd_all = (w.astype(BF16) for w in (w_ffn_gate, w_ffn_up, w_ffn_down))
    w_in_all, w_out_all = w_in.astype(BF16), w_out.astype(BF16)

    for layer in range(depth):
        last = layer == depth - 1
        cv = jnp.zeros((SUBLANES, d), F32).at[:bsz].set(c).at[bsz].set(c_ctx)
        mod = _adaln(cv, w_mod[layer], b_mod[layer]).reshape(SUBLANES, N_MOD, d)
        mod_x = mod[:bsz]
        mod_c = jnp.broadcast_to(mod[bsz:bsz + 1], (bsz, N_MOD, d))

        ffn = functools.partial(_ffn, wg=wg_all, wu=wu_all, wd=wd_all, layer=layer)

        h = _prenorm(x, norm_pre[layer, 0], mod_x[:, 0], mod_x[:, 1])
        y = ffn(h.reshape(bsz * n, d), slot=0)
        x, hx = _resid(x, y, norm_post[layer, 0], mod_x[:, 2], 0.5,
                       nxt=(norm_pre[layer, 1], mod_x[:, 3], mod_x[:, 4]))
        hcx = _prenorm(ctx, norm_pre[layer, 0], mod_c[:, 0], mod_c[:, 1])
        yc = ffn(hcx.reshape(bsz * lc, d), slot=0)
        ctx, hc = _resid(ctx, yc, norm_post[layer, 0], mod_c[:, 2], 0.5,
                         nxt=(norm_pre[layer, 1], mod_c[:, 3], mod_c[:, 4]))

        w_dt = jnp.pad(w_in_all[layer:layer + 1, :, dt_col0:], ((0, 0), (0, 0), (0, LANES - 2 * heads)))
        hx2, hc2 = hx.reshape(bsz * n, d), hc.reshape(bsz * lc, d)
        q_tiles = (0, attn_w // tile)
        k_tiles = (ctx_col0 // tile, (ctx_col0 + kv_w) // tile)
        a_lat = _proj(hx2, w_in_all, layer, BF16, n_cols=xbc_col0,
                      rope=(rope_tabs, n, q_tiles, k_tiles)).reshape(bsz, n, xbc_col0)
        xbc_lat = _proj(hx2, w_in_all, layer, BF16, col0=xbc_col0, n_cols=xbc_w).reshape(bsz, n, xbc_w)
        dt_lat = _proj(hx2, w_dt, 0, F32).reshape(bsz, n, LANES)
        kv_ctx = _proj(hc2, w_in_all, layer, BF16, col0=ctx_col0, n_cols=2 * kv_w).reshape(bsz, lc, 2 * kv_w)
        xbc_ctx = _proj(hc2, w_in_all, layer, BF16, col0=xbc_col0, n_cols=xbc_w).reshape(bsz, lc, xbc_w)
        dt_ctx = _proj(hc2, w_dt, 0, F32).reshape(bsz, lc, LANES)

        attn = _attention(a_lat, kv_ctx, attn_sink[layer].astype(F32), attn_w, ctx_col0, ctx_col0 + kv_w)
        xbc_lat = _conv_silu(xbc_lat, conv_w[layer], conv_b[layer])
        xbc_ctx = _conv_silu(xbc_ctx, conv_w[layer], conv_b[layer])
        y_f, y_b = _ssd(xbc_lat, dt_lat, xbc_ctx, dt_ctx, a_log[layer], dt_bias[layer], inner, heads)
        d_exp = jnp.repeat(d_skip[layer].astype(F32), SSM_HEAD_DIM)
        merged = _merge(attn, y_f, y_b, xbc_lat, a_lat, attn_w, d_exp, attn_norm[layer], ssm_norm[layer])
        y = _proj(merged.reshape(bsz * n, attn_w + inner), w_out_all, layer, BF16)
        x, h = _resid(x, y, norm_post[layer, 1], mod_x[:, 5], 1.0,
                      nxt=(norm_pre[layer, 2], mod_x[:, 6], mod_x[:, 7]))

        y = ffn(h.reshape(bsz * n, d), slot=1)
        x = _resid(x, y, norm_post[layer, 2], mod_x[:, 8], 0.5)
        if not last:
            raise NotImplementedError("context output path is only needed for depth > 1")
    return x
```

```python
import functools

import jax
import jax.numpy as jnp
from jax import lax
from jax.experimental import pallas as pl
from jax.experimental.pallas import tpu as pltpu

GRID_W = 64
HEAD_DIM = 128
N_HEADS = 16
N_KV_HEADS = 4
GQA_GROUP = N_HEADS // N_KV_HEADS
BLOCK = 128
ROPE_BASE = 10000.0
ROPE_PAIRS = HEAD_DIM // 4
SSM_HEAD_DIM = 64
SSM_GROUPS = 4
SSM_STATE = 128
CONV_K = 5
CHUNK = 128
N_MOD = 9
EPS = 1e-6
NEG_INF = -1e30

LANES = 128
SUBLANES = 8
MIB = 1024 * 1024

F32 = jnp.float32
BF16 = jnp.bfloat16


def _params(semantics, vmem_mib):
    return pltpu.CompilerParams(dimension_semantics=semantics, vmem_limit_bytes=vmem_mib * MIB)


def _silu(t):
    return t * jax.nn.sigmoid(t)


def _rms_scale(t):
    return lax.rsqrt(jnp.mean(t * t, axis=-1, keepdims=True) + EPS)


def _mod_kernel(c_ref, w_ref, b_ref, o_ref):
    s = _silu(c_ref[...]).astype(BF16)
    o_ref[...] = jnp.dot(s, w_ref[...].astype(BF16), preferred_element_type=F32) + b_ref[...]


def _adaln(cv, w_mod, b_mod):
    rows, d = cv.shape
    n = w_mod.shape[1]
    tn = 512
    return pl.pallas_call(
        _mod_kernel,
        grid=(n // tn,),
        in_specs=[
            pl.BlockSpec((rows, d), lambda j: (0, 0)),
            pl.BlockSpec((d, tn), lambda j: (0, j)),
            pl.BlockSpec((1, tn), lambda j: (0, j)),
        ],
        out_specs=pl.BlockSpec((rows, tn), lambda j: (0, j)),
        out_shape=jax.ShapeDtypeStruct((rows, n), F32),
        compiler_params=_params(("parallel",), 40),
        name="adaln",
    )(cv, w_mod, b_mod.reshape(1, n))


def _prenorm_kernel(s_ref, g_ref, sh_ref, sc_ref, h_ref):
    s = s_ref[0]
    h = s * _rms_scale(s) * g_ref[...]
    h_ref[0] = (h * (1.0 + sc_ref[0]) + sh_ref[0]).astype(BF16)


def _prenorm(s, g_pre, shift, scale, tr=256):
    b, n, d = s.shape
    row = pl.BlockSpec((1, tr, d), lambda i, r: (i, r, 0))
    vec = pl.BlockSpec((1, 1, d), lambda i, r: (i, 0, 0))
    return pl.pallas_call(
        _prenorm_kernel,
        grid=(b, n // tr),
        in_specs=[row, pl.BlockSpec((1, d), lambda i, r: (0, 0)), vec, vec],
        out_specs=row,
        out_shape=jax.ShapeDtypeStruct((b, n, d), BF16),
        compiler_params=_params(("parallel", "parallel"), 32),
        name="prenorm",
    )(s, g_pre.reshape(1, d), shift.reshape(b, 1, d), scale.reshape(b, 1, d))


def _resid_kernel(*refs, coef, with_h):
    if with_h:
        s_ref, y_ref, gp_ref, gate_ref, gn_ref, sh_ref, sc_ref, o_ref, h_ref = refs
    else:
        s_ref, y_ref, gp_ref, gate_ref, o_ref = refs
    y = y_ref[0].astype(F32)
    s_new =s_ref[0] + (coef * gate_ref[0]) * (y * _rms_scale(y) * gp_ref[...])
    o_ref[0] = s_new
    if with_h:
        h = s_new * _rms_scale(s_new) * gn_ref[...]
        h_ref[0] = (h * (1.0 + sc_ref[0]) + sh_ref[0]).astype(BF16)


def _resid(s, y, g_post, gate, coef, nxt=None, tr=256):
    b, n, d = s.shape
    row = pl.BlockSpec((1, tr, d), lambda i, r: (i, r, 0))
    vec = pl.BlockSpec((1, 1, d), lambda i, r: (i, 0, 0))
    par = pl.BlockSpec((1, d), lambda i, r: (0, 0))
    args = [s, y.reshape(b, n, d), g_post.reshape(1, d), gate.reshape(b, 1, d)]
    in_specs = [row, row, par, vec]
    out_shape = [jax.ShapeDtypeStruct((b, n, d), F32)]
    out_specs = [row]
    if nxt is not None:
        g_next, shift, scale = nxt
        args += [g_next.reshape(1, d), shift.reshape(b, 1, d), scale.reshape(b, 1, d)]
        in_specs += [par, vec, vec]
        out_shape.append(jax.ShapeDtypeStruct((b, n, d), BF16))
        out_specs.append(row)
    out = pl.pallas_call(
        functools.partial(_resid_kernel, coef=coef, with_h=nxt is not None),
        grid=(b, n // tr),
        in_specs=in_specs,
        out_specs=out_specs,
        out_shape=out_shape,
        compiler_params=_params(("parallel", "parallel"), 48),
        name="resid",
    )(*args)
    return out if nxt is not None else out[0]


def _ffn_kernel(h_ref, wg_ref, wu_ref, wd_ref, y_ref, acc_ref, *, n_chunk):
    j = pl.program_id(1)

    @pl.when(j == 0)
    def _():
        acc_ref[...] = jnp.zeros_like(acc_ref)

    h = h_ref[...]
    g = jnp.dot(h, wg_ref[...].astype(BF16), preferred_element_type=F32)
    u = jnp.dot(h, wu_ref[...].astype(BF16), preferred_element_type=F32)
    a = (_silu(g) * u).astype(BF16)
    d = acc_ref.shape[1]
    for n0 in range(0, d, n_chunk):
        wd = wd_ref[:, n0:n0 + n_chunk].astype(BF16)
        acc_ref[:, n0:n0 + n_chunk] += jnp.dot(a, wd, preferred_element_type=F32)

    @pl.when(j == pl.num_programs(1) - 1)
    def _():
        y_ref[...] = acc_ref[...].astype(y_ref.dtype)


def _ffn(h, wg, wu, wd, layer, slot, tm=1024, tf=256):
    r, d = h.shape
    f = wg.shape[3]
    tm = min(tm, r)
    once = pl.Buffered(1)
    return pl.pallas_call(
        functools.partial(_ffn_kernel, n_chunk=1024),
        grid=(r // tm, f // tf),
        in_specs=[
            pl.BlockSpec((tm, d), lambda i, j: (i, 0), pipeline_mode=once),
            pl.BlockSpec((None, None, d, tf), lambda i, j: (layer, slot, 0, j)),
            pl.BlockSpec((None, None, d, tf), lambda i, j: (layer, slot, 0, j)),
            pl.BlockSpec((None, None, tf, d), lambda i, j: (layer, slot, j, 0)),
        ],
        out_specs=pl.BlockSpec((tm, d), lambda i, j: (i, 0), pipeline_mode=once),
        out_shape=jax.ShapeDtypeStruct((r, d), BF16),
        scratch_shapes=[pltpu.VMEM((tm, d), F32)],
        compiler_params=_params(("parallel", "arbitrary"), 58),
        name="ffn",
    )(h, wg, wu, wd)


def _proj_kernel(*refs, q_tiles, k_tiles):
    rope = q_tiles is not None
    if rope:
        h_ref, w_ref, cos_ref, sa_ref, sb_ref, o_ref = refs
    else:
        h_ref, w_ref, o_ref = refs
    if not rope:
        o_ref[...] = jnp.dot(h_ref[...], w_ref[...].astype(BF16), preferred_element_type=F32).astype(o_ref.dtype)
        return
    j = pl.program_id(1)
    is_q = (j >= q_tiles[0]) & (j < q_tiles[1])
    is_rope = is_q | ((j >= k_tiles[0]) & (j < k_tiles[1]))

    @pl.when(is_rope)
    def _():
        scale = jnp.where(is_q, HEAD_DIM ** -0.5, 1.0).astype(F32)
        cos, sa, sb = cos_ref[...] * scale, sa_ref[...] * scale, sb_ref[...] * scale
        h = h_ref[...]
        slab = 2 * HEAD_DIM
        for c0 in range(0, o_ref.shape[1], slab):
            res = jnp.dot(h, w_ref[:, c0:c0 + slab].astype(BF16), preferred_element_type=F32)
            for k in range(slab // HEAD_DIM):
                t = res[:, k * HEAD_DIM:(k + 1) * HEAD_DIM]
                o = t * cos + pltpu.roll(t, ROPE_PAIRS, 1) * sa + pltpu.roll(t, HEAD_DIM - ROPE_PAIRS, 1) * sb
                o_ref[:, c0 + k * HEAD_DIM:c0 + (k + 1) * HEAD_DIM] = o.astype(o_ref.dtype)

    @pl.when(jnp.logical_not(is_rope))
    def _():
        o_ref[...] = jnp.dot(h_ref[...], w_ref[...].astype(BF16), preferred_element_type=F32).astype(o_ref.dtype)


def _proj(h, w, layer, out_dtype, *, col0=0, n_cols=None, tm=1024, tn=512, rope=None):
    r, kdim = h.shape
    n_cols = w.shape[2] - col0 if n_cols is None else n_cols
    tm = min(tm, r)
    tn = min(tn, n_cols)
    jo = col0 // tn
    in_specs = [
        pl.BlockSpec((tm, kdim), lambda i, j: (i, 0)),
        pl.BlockSpec((None, kdim, tn), lambda i, j: (layer, 0, j + jo)),
    ]
    args = [h, w]
    q_tiles = k_tiles = None
    if rope is not None:
        tables, seq_len, q_tiles, k_tiles = rope
        per_seq = seq_len // tm
        tab = pl.BlockSpec((tm, HEAD_DIM), lambda i, j: (i % per_seq, 0))
        in_specs += [tab, tab, tab]
        args += list(tables)
    return pl.pallas_call(
        functools.partial(_proj_kernel, q_tiles=q_tiles, k_tiles=k_tiles),
        grid=(r // tm, n_cols // tn),
        in_specs=in_specs,
        out_specs=pl.BlockSpec((tm, tn), lambda i, j: (i, j)),
        out_shape=jax.ShapeDtypeStruct((r, n_cols), out_dtype),
        compiler_params=_params(("parallel", "arbitrary"), 52),
        name="proj",
    )(*args)


def _rope_tables(n):
    row = (jnp.arange(n) // GRID_W).astype(F32)
    col = (jnp.arange(n) % GRID_W).astype(F32)
    inv = ROPE_BASE ** (-jnp.arange(ROPE_PAIRS, dtype=F32) / ROPE_PAIRS)
    ar = row[:, None] * inv
    ac = col[:, None] * inv
    ang = jnp.concatenate([ar, ar, ac, ac], axis=-1)
    cos, sin = jnp.cos(ang), jnp.sin(ang)
    upper = (jnp.arange(HEAD_DIM) % (2 * ROPE_PAIRS)) >= ROPE_PAIRS
    return cos, jnp.where(upper, sin, 0.0), jnp.where(upper, 0.0, -sin)


def _conv_kernel(x_ref, w_ref, b_ref, o_ref, pad_ref, *, rows):
    n = x_ref.shape[1]
    tc = x_ref.shape[2]
    halo = CONV_K // 2
    zeros = jnp.zeros((SUBLANES, tc), F32)
    pad_ref[0:SUBLANES, :] = zeros
    pad_ref[n + SUBLANES:n + 2 * SUBLANES, :] = zeros
    pad_ref[SUBLANES:n + SUBLANES, :] = x_ref[0].astype(F32)
    w = w_ref[...]
    bias = b_ref[...]
    for r0 in range(0, n, rows):
        acc = jnp.broadcast_to(bias, (rows, tc))
        for k in range(CONV_K):
            start = SUBLANES + r0 + k - halo
            acc = acc + pad_ref[start:start + rows, :] * w[k:k + 1, :]
        o_ref[0, r0:r0 + rows, :] = _silu(acc).astype(o_ref.dtype)


def _conv_silu(u, w, bias, tc=128):
    b, n, c = u.shape
    rows = min(512, n)
    return pl.pallas_call(
        functools.partial(_conv_kernel, rows=rows),
        grid=(b, c // tc),
        in_specs=[
            pl.BlockSpec((1, n, tc), lambda i, j: (i, 0, j)),
            pl.BlockSpec((CONV_K, tc), lambda i, j: (0, j)),
            pl.BlockSpec((1, tc), lambda i, j: (0, j)),
        ],
        out_specs=pl.BlockSpec((1, n, tc), lambda i, j: (i, 0, j)),
        out_shape=jax.ShapeDtypeStruct((b, n, c), BF16),
        scratch_shapes=[pltpu.VMEM((n + 2 * SUBLANES, tc), F32)],
        compiler_params=_params(("parallel", "parallel"), 40),
        name="conv_silu",
    )(u, w, bias.reshape(1, c))


def _attn_kernel(sink_ref, q_ref, kp_ref, ko_ref, kn_ref, vp_ref, vo_ref, vn_ref, kc_ref, vc_ref, o_ref):
    n = pl.program_id(1)
    nb = pl.num_programs(1)
    rows = GQA_GROUP * BLOCK
    qi = lax.broadcasted_iota(jnp.int32, (rows, BLOCK), 0) & (BLOCK - 1)
    kj = lax.broadcasted_iota(jnp.int32, (rows, BLOCK), 1)
    prev_ok = kj >= qi + jnp.where(n > 0, 0, BLOCK)
    next_ok = kj <= qi - jnp.where(n < nb - 1, 0, BLOCK)
    for kv in range(N_KV_HEADS):
        heads = range(kv * GQA_GROUP, (kv + 1) * GQA_GROUP)
        cols = slice(kv * HEAD_DIM, (kv + 1) * HEAD_DIM)
        qs = jnp.concatenate([q_ref[0, :, h * HEAD_DIM:(h + 1) * HEAD_DIM] for h in heads], axis=0)
        kb = jnp.concatenate([kp_ref[0, :, cols], ko_ref[0, :, cols], kn_ref[0, :, cols], kc_ref[0, :, cols]], axis=0)
        vb = jnp.concatenate([vp_ref[0, :, cols], vo_ref[0, :, cols], vn_ref[0, :, cols], vc_ref[0, :, cols]], axis=0)
        s = lax.dot_general(qs, kb, (((1,), (1,)), ((), ())), preferred_element_type=F32)
        s = jnp.concatenate([
            jnp.where(prev_ok, s[:, 0:BLOCK], NEG_INF), s[:, BLOCK:2 * BLOCK],
            jnp.where(next_ok, s[:, 2 * BLOCK:3 * BLOCK], NEG_INF), s[:, 3 * BLOCK:]], axis=1)
        sink = jnp.concatenate([jnp.full((BLOCK, 1), sink_ref[h], F32) for h in heads], axis=0)
        m = jnp.maximum(jnp.max(s, axis=-1, keepdims=True), sink)
        p = jnp.exp(s - m)
        denom = jnp.sum(p, axis=-1, keepdims=True) + jnp.exp(sink - m)
        o = jnp.dot(p.astype(BF16), vb, preferred_element_type=F32) / denom
        for g, h in enumerate(heads):
            o_ref[0, :, h * HEAD_DIM:(h + 1) * HEAD_DIM] = o[g * BLOCK:(g + 1) * BLOCK].astype(o_ref.dtype)


def _attention(a, kvc, sink, q_width, k_col0, v_col0):
    b, n, _ = a.shape
    lc = kvc.shape[1]
    nb = n // BLOCK
    kw = N_KV_HEADS * HEAD_DIM
    kb0, vb0 = k_col0 // kw, v_col0 // kw

    def kv_spec(col_blk, shift):
        return pl.BlockSpec((1, BLOCK, kw), lambda i, j: (i, jnp.clip(j + shift, 0, nb - 1), col_blk))

    return pl.pallas_call(
        _attn_kernel,
        grid=(b, nb),
        in_specs=[
            pl.BlockSpec(memory_space=pltpu.SMEM),
            pl.BlockSpec((1, BLOCK, q_width), lambda i, j: (i, j, 0)),
            kv_spec(kb0, -1), kv_spec(kb0, 0), kv_spec(kb0, 1),
            kv_spec(vb0, -1), kv_spec(vb0, 0), kv_spec(vb0, 1),
            pl.BlockSpec((1, lc, kw), lambda i, j: (i, 0, 0)),
            pl.BlockSpec((1, lc, kw), lambda i, j: (i, 0, 1)),
        ],
        out_specs=pl.BlockSpec((1, BLOCK, q_width), lambda i, j: (i, j, 0)),
        out_shape=jax.ShapeDtypeStruct((b, n, q_width), BF16),
        compiler_params=_params(("parallel", "arbitrary"), 32),
        name="window_attn",
    )(sink, a, a, a, a, a, a, a, kvc, kvc)


def _split_bf16(t, parts):
    out = []
    for _ in range(parts):
        hi = t.astype(BF16)
        out.append(hi)
        t = t - hi.astype(F32)
    return out


def _ssd_kernel(*refs, n_ctx_chunks, heads, inner):
    (xlf, blf, clf, dlf, xlb, blb, clb, dlb,
     xcf, bcf, ccf, dcf, xcb, bcb, ccb, dcb,
     alog_ref, bias_ref, e_ref, yf_ref, yb_ref,
     x_s, b_s, c_s, dt_s, h_s) = refs
    step = pl.program_id(1)
    hpg = heads // SSM_GROUPS
    gw = hpg * SSM_HEAD_DIM
    is_ctx = step < n_ctx_chunks

    @pl.when(step == 0)
    def _():
        h_s[...] = jnp.zeros_like(h_s)

    @pl.when(is_ctx)
    def _():
        for d, (xr, br, cr, dr) in enumerate(((xcf, bcf, ccf, dcf), (xcb, bcb, ccb, dcb))):
            x_s[d], b_s[d], c_s[d], dt_s[d] = xr[0], br[0], cr[0], dr[0]

    @pl.when(jnp.logical_not(is_ctx))
    def _():
        for d, (xr, br, cr, dr) in enumerate(((xlf, blf, clf, dlf), (xlb, blb, clb, dlb))):
            x_s[d], b_s[d], c_s[d], dt_s[d] = xr[0], br[0], cr[0], dr[0]

    lane = lax.broadcasted_iota(jnp.int32, (CHUNK, LANES), 1)
    row = lax.broadcasted_iota(jnp.int32, (CHUNK, CHUNK), 0)
    col = lax.broadcasted_iota(jnp.int32, (CHUNK, CHUNK), 1)
    is_fwd = lane < heads
    live = lane < 2 * heads

    dt_raw = jnp.where(is_fwd, dt_s[0], dt_s[1])
    dtv = jax.nn.softplus(dt_raw + bias_ref[...])
    a_row = jnp.where(live[0:1], -jnp.exp(alog_ref[...]), 0.0)
    da = dtv * a_row
    tri = jnp.where(row >= col, 1.0, 0.0).astype(BF16)
    acs = sum(jnp.dot(tri, part, preferred_element_type=F32) for part in _split_bf16(da, 3))
    tot = acs[CHUNK - 1:CHUNK, :]
    u = acs - jnp.where(is_fwd, 0.0, da)
    e_u = jnp.exp(u)
    e_r = jnp.exp(tot - u)
    rs = jnp.where(is_fwd, e_u, e_r)
    wst = jnp.where(is_fwd, e_r, e_u) * dtv
    packed = jnp.where(lane < 2 * heads, rs, pltpu.roll(wst, 2 * heads, 1))
    hi, lo = _split_bf16(packed, 2)
    packed2 = jnp.concatenate([hi, lo], axis=1)

    def expand(k):
        return jnp.dot(packed2, e_ref[:, k * inner:(k + 1) * inner], preferred_element_type=F32)

    us = jnp.where(is_fwd, u, -u)
    us_t = us.T
    dt_t = dtv.T

    for d in range(2):
        rs_x = expand(d)
        w_x = expand(2 + d)
        dec = rs_x[CHUNK - 1:CHUNK, :] if d == 0 else rs_x[0:1, :]
        keep = (row >= col) if d == 0 else (col >= row)
        y_ref = yf_ref if d == 0 else yb_ref
        for g in range(SSM_GROUPS):
            bm = b_s[d, :, g * SSM_STATE:(g + 1) * SSM_STATE]
            cm = c_s[d, :, g * SSM_STATE:(g + 1) * SSM_STATE]
            xg = x_s[d, :, g * gw:(g + 1) * gw]
            cb = lax.dot_general(cm, bm, (((1,), (1,)), ((), ())), preferred_element_type=F32)
            h_old = h_s[d, :, g * gw:(g + 1) * gw]
            y_off = jnp.dot(cm, h_old.astype(BF16), preferred_element_type=F32) * rs_x[:, g * gw:(g + 1) * gw]
            pieces = []
            for r in range(0, hpg, 2):
                mats = []
                for hh in (g * hpg + r, g * hpg + r + 1):
                    ln = hh + d * heads
                    diff = us[:, ln:ln + 1] - us_t[ln:ln + 1, :]
                    decay = jnp.exp(jnp.where(keep, diff, -jnp.inf))
                    mats.append((cb * decay * dt_t[ln:ln + 1, :]).astype(BF16))
                xp = xg[:, r * SSM_HEAD_DIM:(r + 2) * SSM_HEAD_DIM]
                left = lane < SSM_HEAD_DIM
                zero = jnp.zeros_like(xp)
                rhs = jnp.concatenate([jnp.where(left, xp, zero), jnp.where(left, zero, xp)], axis=0)
                pieces.append(jnp.dot(jnp.concatenate(mats, axis=1), rhs, preferred_element_type=F32))
            y_g = jnp.concatenate(pieces, axis=1) + y_off
            y_ref[0, :, g * gw:(g + 1) * gw] = y_g.astype(y_ref.dtype)

            xw = (xg.astype(F32) * w_x[:, g * gw:(g + 1) * gw]).astype(BF16)
            st = lax.dot_general(bm, xw, (((0,), (0,)), ((), ())), preferred_element_type=F32)
            h_s[d, :, g * gw:(g + 1) * gw] = h_old * dec[:, g * gw:(g + 1) * gw] + st


def _ssd(xbc_lat, dt_lat, xbc_ctx, dt_ctx, a_log, dt_bias, inner, heads):
    b, n, _ = xbc_lat.shape
    lc = xbc_ctx.shape[1]
    nlat, ncc = n // CHUNK, lc // CHUNK
    bc = SSM_GROUPS * SSM_STATE
    assert 4 * heads == LANES and inner % bc == 0
    b_blk, c_blk = inner // bc, inner // bc + 1

    def lat_f(s):
        return jnp.maximum(s - ncc, 0)

    def lat_b(s):
        return nlat - 1 - jnp.maximum(s - ncc, 0)

    def ctx_f(s):
        return jnp.minimum(s, ncc - 1)

    def ctx_b(s):
        return jnp.maximum(ncc - 1 - s, 0)

    def specs(chunk_of):
        return [
            pl.BlockSpec((1, CHUNK, inner), lambda i, s: (i, chunk_of(s), 0)),
            pl.BlockSpec((1, CHUNK, bc), lambda i, s: (i, chunk_of(s), b_blk)),
            pl.BlockSpec((1, CHUNK, bc), lambda i, s: (i, chunk_of(s), c_blk)),
        ]

    def dt_spec(chunk_of):
        return [pl.BlockSpec((1, CHUNK, LANES), lambda i, s: (i, chunk_of(s), 0))]

    in_specs = (specs(lat_f) + dt_spec(lat_f) + specs(lat_b) + dt_spec(lat_b)
                + specs(ctx_f) + dt_spec(ctx_f) + specs(ctx_b) + dt_spec(ctx_b))
    const = lambda shape: pl.BlockSpec(shape, lambda i, s: (0, 0))
    in_specs += [const((1, LANES)), const((1, LANES)), const((2 * LANES, 4 * inner))]

    pad = jnp.zeros((LANES - 2 * heads,), F32)
    alog_row = jnp.concatenate([a_log.reshape(-1).astype(F32), pad]).reshape(1, LANES)
    bias_row = jnp.concatenate([dt_bias.reshape(-1).astype(F32), pad]).reshape(1, LANES)
    cidx = jnp.arange(4 * inner)
    src = heads * (cidx // inner) + (cidx % inner) // SSM_HEAD_DIM
    expand = (jnp.arange(LANES)[:, None] == src[None, :]).astype(BF16)
    expand = jnp.concatenate([expand, expand], axis=0)

    lat = [xbc_lat, xbc_lat, xbc_lat, dt_lat]
    ctx = [xbc_ctx, xbc_ctx, xbc_ctx, dt_ctx]
    y_shape = jax.ShapeDtypeStruct((b, n, inner), BF16)
    return pl.pallas_call(
        functools.partial(_ssd_kernel, n_ctx_chunks=ncc, heads=heads, inner=inner),
        grid=(b, ncc + nlat),
        in_specs=in_specs,
        out_specs=[
            pl.BlockSpec((1, CHUNK, inner), lambda i, s: (i, lat_f(s), 0)),
            pl.BlockSpec((1, CHUNK, inner), lambda i, s: (i, lat_b(s), 0)),
        ],
        out_shape=[y_shape, y_shape],
        scratch_shapes=[
            pltpu.VMEM((2, CHUNK, inner), BF16),
            pltpu.VMEM((2, CHUNK, bc), BF16),
            pltpu.VMEM((2, CHUNK, bc), BF16),
            pltpu.VMEM((2, CHUNK, LANES), F32),
            pltpu.VMEM((2, SSM_STATE, inner), F32),
        ],
        compiler_params=_params(("parallel", "arbitrary"), 48),
        name="ssd_scan",
    )(*lat, *lat, *ctx, *ctx, alog_row, bias_row, expand)


def _merge_kernel(attn_ref, yf_ref, yb_ref, xs_ref, z_ref, d_ref, an_ref, sn_ref, o_ref):
    aw = attn_ref.shape[2]
    a = attn_ref[0].astype(F32)
    o_ref[0, :, 0:aw] = (a * _rms_scale(a) * an_ref[...]).astype(BF16)
    y = yf_ref[0].astype(F32) + yb_ref[0].astype(F32) + d_ref[...] * xs_ref[0].astype(F32)
    t = y * _silu(z_ref[0].astype(F32))
    o_ref[0, :, aw:] = (t * _rms_scale(t) * sn_ref[...]).astype(BF16)


def _merge(attn, y_f, y_b, xbc, a, z_col0, d_exp, attn_norm, ssm_norm, tr=256):
    b, n, aw = attn.shape
    inner = y_f.shape[2]
    row = lambda w, cb=0: pl.BlockSpec((1, tr, w), lambda i, r: (i, r, cb))
    par = lambda w: pl.BlockSpec((1, w), lambda i, r: (0, 0))
    return pl.pallas_call(
        _merge_kernel,
        grid=(b, n // tr),
        in_specs=[row(aw), row(inner), row(inner), row(inner), row(inner, z_col0 // inner),
                  par(inner), par(aw), par(inner)],
        out_specs=row(aw + inner),
        out_shape=jax.ShapeDtypeStruct((b, n, aw + inner), BF16),
        compiler_params=_params(("parallel", "parallel"), 40),
        name="merge",
    )(attn, y_f, y_b, xbc, a, d_exp.reshape(1, inner), attn_norm.reshape(1, aw), ssm_norm.reshape(1, inner))


def kernel(x, c, ctx, c_ctx, w_mod, b_mod, norm_pre, norm_post, w_ffn_gate, w_ffn_up, w_ffn_down,
           w_in, attn_sink, attn_norm, conv_w, conv_b, a_log, dt_bias, d_skip, ssm_norm, w_out):
    bsz, n, d = x.shape
    lc = ctx.shape[1]
    depth = w_mod.shape[0]
    attn_w = N_HEADS * HEAD_DIM
    inner = d // 2
    heads = inner // SSM_HEAD_DIM
    kv_w = N_KV_HEADS * HEAD_DIM
    bc_w = SSM_GROUPS * SSM_STATE
    xbc_w = inner + 2 * bc_w
    ctx_col0 = attn_w + inner
    xbc_col0 = ctx_col0 + 2 * kv_w
    dt_col0 = xbc_col0 + xbc_w
    rope_tabs = _rope_tables(n)
    tile = 512
    wg_all, wu_all, wd_all = w_ffn_gate.astype(BF16), w_ffn_up, w_ffn_down
    w_in_all, w_out_all = w_in, w_out

    for layer in range(depth):
        last = layer == depth - 1
        cv = jnp.zeros((SUBLANES, d), F32).at[:bsz].set(c).at[bsz].set(c_ctx)
        mod = _adaln(cv, w_mod[layer], b_mod[layer]).reshape(SUBLANES, N_MOD, d)
        mod_x = mod[:bsz]
        mod_c = jnp.broadcast_to(mod[bsz:bsz + 1], (bsz, N_MOD, d))

        ffn = functools.partial(_ffn, wg=wg_all, wu=wu_all, wd=wd_all, layer=layer)

        h = _prenorm(x, norm_pre[layer, 0], mod_x[:, 0], mod_x[:, 1])
        y = ffn(h.reshape(bsz * n, d), slot=0)
        x, hx = _resid(x, y, norm_post[layer, 0], mod_x[:, 2], 0.5,
                       nxt=(norm_pre[layer, 1], mod_x[:, 3], mod_x[:, 4]))
        hcx = _prenorm(ctx, norm_pre[layer, 0], mod_c[:, 0], mod_c[:, 1])
        yc = ffn(hcx.reshape(bsz * lc, d), slot=0)
        ctx, hc = _resid(ctx, yc, norm_post[layer, 0], mod_c[:, 2], 0.5,
                         nxt=(norm_pre[layer, 1], mod_c[:, 3], mod_c[:, 4]))

        w_dt = jnp.pad(w_in_all[layer:layer + 1, :, dt_col0:], ((0, 0), (0, 0), (0, LANES - 2 * heads)))
        hx2, hc2 = hx.reshape(bsz * n, d), hc.reshape(bsz * lc, d)
        q_tiles = (0, attn_w // tile)
        k_tiles = (ctx_col0 // tile, (ctx_col0 + kv_w) // tile)
        a_lat = _proj(hx2, w_in_all, layer, BF16, n_cols=xbc_col0,
                      rope=(rope_tabs, n, q_tiles, k_tiles)).reshape(bsz, n, xbc_col0)
        xbc_lat = _proj(hx2, w_in_all, layer, BF16, col0=xbc_col0, n_cols=xbc_w).reshape(bsz, n, xbc_w)
        dt_lat = _proj(hx2, w_dt, 0, F32).reshape(bsz, n, LANES)
        kv_ctx = _proj(hc2, w_in_all, layer, BF16, col0=ctx_col0, n_cols=2 * kv_w).reshape(bsz, lc, 2 * kv_w)
        xbc_ctx = _proj(hc2, w_in_all, layer, BF16, col0=xbc_col0, n_cols=xbc_w).reshape(bsz, lc, xbc_w)
        dt_ctx = _proj(hc2, w_dt, 0, F32).reshape(bsz, lc, LANES)

        attn = _attention(a_lat, kv_ctx, attn_sink[layer].astype(F32), attn_w, ctx_col0, ctx_col0 + kv_w)
        xbc_lat = _conv_silu(xbc_lat, conv_w[layer], conv_b[layer])
        xbc_ctx = _conv_silu(xbc_ctx, conv_w[layer], conv_b[layer])
        y_f, y_b = _ssd(xbc_lat, dt_lat, xbc_ctx, dt_ctx, a_log[layer], dt_bias[layer], inner, heads)
        d_exp = jnp.repeat(d_skip[layer].astype(F32), SSM_HEAD_DIM)
        merged = _merge(attn, y_f, y_b, xbc_lat, a_lat, attn_w, d_exp, attn_norm[layer], ssm_norm[layer])
        y = _proj(merged.reshape(bsz * n, attn_w + inner), w_out_all, layer, BF16)
        x, h = _resid(x, y, norm_post[layer, 1], mod_x[:, 5], 1.0,
                      nxt=(norm_pre[layer, 2], mod_x[:, 6], mod_x[:, 7]))

        y = ffn(h.reshape(bsz * n, d), slot=1)
        x = _resid(x, y, norm_post[layer, 2], mod_x[:, 8], 0.5)
        if not last:
            raise NotImplementedError("context output path is only needed for depth > 1")
    return x
```

```python
import functools

import jax
import jax.numpy as jnp
from jax import lax
from jax.experimental import pallas as pl
from jax.experimental.pallas import tpu as pltpu

GRID_W = 64
HEAD_DIM = 128
N_HEADS = 16
N_KV_HEADS = 4
GQA_GROUP = N_HEADS // N_KV_HEADS
BLOCK = 128
ROPE_BASE = 10000.0
ROPE_PAIRS = HEAD_DIM // 4
SSM_HEAD_DIM = 64
SSM_GROUPS = 4
SSM_STATE = 128
CONV_K = 5
CHUNK = 128
N_MOD = 9
EPS = 1e-6
NEG_INF = -1e30
LOG2_E = 1.4426950408889634

LANES = 128
SUBLANES = 8
MIB = 1024 * 1024

F32 = jnp.float32
BF16 = jnp.bfloat16


def _params(semantics, vmem_mib):
    return pltpu.CompilerParams(dimension_semantics=semantics, vmem_limit_bytes=vmem_mib * MIB)


def _silu(t):
    return t * jax.nn.sigmoid(t)


def _rms_scale(t):
    return lax.rsqrt(jnp.mean(t * t, axis=-1, keepdims=True) + EPS)


def _mod_kernel(c_ref, w_ref, b_ref, o_ref):
    s = _silu(c_ref[...]).astype(BF16)
    o_ref[...] = jnp.dot(s, w_ref[...].astype(BF16), preferred_element_type=F32) + b_ref[...]


def _adaln(cv, w_mod, b_mod):
    rows, d = cv.shape
    n = w_mod.shape[1]
    tn = 512
    return pl.pallas_call(
        _mod_kernel,
        grid=(n // tn,),
        in_specs=[
            pl.BlockSpec((rows, d), lambda j: (0, 0)),
            pl.BlockSpec((d, tn), lambda j: (0, j)),
            pl.BlockSpec((1, tn), lambda j: (0, j)),
        ],
        out_specs=pl.BlockSpec((rows, tn), lambda j: (0, j)),
        out_shape=jax.ShapeDtypeStruct((rows, n), F32),
        compiler_params=_params(("parallel",), 40),
        name="adaln",
    )(cv, w_mod, b_mod.reshape(1, n))


def _prenorm_kernel(s_ref, g_ref, sh_ref, sc_ref, h_ref):
    s = s_ref[0]
    h = s * _rms_scale(s) * g_ref[...]
    h_ref[0] = (h * (1.0 + sc_ref[0]) + sh_ref[0]).astype(BF16)


def _prenorm(s, g_pre, shift, scale, tr=512):
    b, n, d = s.shape
    tr = min(tr, n)
    row = pl.BlockSpec((1, tr, d), lambda i, r: (i, r, 0))
    vec = pl.BlockSpec((1, 1, d), lambda i, r: (i, 0, 0))
    return pl.pallas_call(
        _prenorm_kernel,
        grid=(b, n // tr),
        in_specs=[row, pl.BlockSpec((1, d), lambda i, r: (0, 0)), vec, vec],
        out_specs=row,
        out_shape=jax.ShapeDtypeStruct((b, n, d), BF16),
        compiler_params=_params(("parallel", "parallel"), 40),
        name="prenorm",
    )(s, g_pre.reshape(1, d), shift.reshape(b, 1, d), scale.reshape(b, 1, d))


def _resid_kernel(*refs, coef, with_h):
    if with_h:
        s_ref, y_ref, gp_ref, gate_ref, gn_ref, sh_ref, sc_ref, o_ref, h_ref = refs
    else:
        s_ref, y_ref, gp_ref, gate_ref, o_ref = refs
    y = y_ref[0].astype(F32)
    s_new =s_ref[0] + (coef * gate_ref[0]) * (y * _rms_scale(y) * gp_ref[...])
    o_ref[0] = s_new
    if with_h:
        h = s_new * _rms_scale(s_new) * gn_ref[...]
        h_ref[0] = (h * (1.0 + sc_ref[0]) + sh_ref[0]).astype(BF16)


def _resid(s, y, g_post, gate, coef, nxt=None):
    b, n, d = s.shape
    tr = min(n, 256 if nxt is not None else 512)
    row = pl.BlockSpec((1, tr, d), lambda i, r: (i, r, 0))
    vec = pl.BlockSpec((1, 1, d), lambda i, r: (i, 0, 0))
    par = pl.BlockSpec((1, d), lambda i, r: (0, 0))
    args = [s, y.reshape(b, n, d), g_post.reshape(1, d), gate.reshape(b, 1, d)]
    in_specs = [row, row, par, vec]
    out_shape = [jax.ShapeDtypeStruct((b, n, d), F32)]
    out_specs = [row]
    if nxt is not None:
        g_next, shift, scale = nxt
        args += [g_next.reshape(1, d), shift.reshape(b, 1, d), scale.reshape(b, 1, d)]
        in_specs += [par, vec, vec]
        out_shape.append(jax.ShapeDtypeStruct((b, n, d), BF16))
        out_specs.append(row)
    out = pl.pallas_call(
        functools.partial(_resid_kernel, coef=coef, with_h=nxt is not None),
        grid=(b, n // tr),
        in_specs=in_specs,
        out_specs=out_specs,
        out_shape=out_shape,
        compiler_params=_params(("parallel", "parallel"), 56),
        name="resid",
    )(*args)
    return out if nxt is not None else out[0]


def _ffn_kernel(h_ref, wg_ref, wu_ref, wd_ref, y_ref, acc_ref, *, n_chunk):
    j = pl.program_id(1)

    @pl.when(j == 0)
    def _():
        acc_ref[...] = jnp.zeros_like(acc_ref)

    h = h_ref[...]
    g = jnp.dot(h, wg_ref[...].astype(BF16), preferred_element_type=F32)
    u = jnp.dot(h, wu_ref[...].astype(BF16), preferred_element_type=F32)
    a = (_silu(g) * u).astype(BF16)
    d = acc_ref.shape[1]
    for n0 in range(0, d, n_chunk):
        wd = wd_ref[:, n0:n0 + n_chunk].astype(BF16)
        acc_ref[:, n0:n0 + n_chunk] += jnp.dot(a, wd, preferred_element_type=F32)

    @pl.when(j == pl.num_programs(1) - 1)
    def _():
        y_ref[...] = acc_ref[...].astype(y_ref.dtype)


def _ffn(h, wg, wu, wd, layer, slot, tm=1024, tf=256):
    r, d = h.shape
    f = wg.shape[3]
    tm = min(tm, r)
    once = pl.Buffered(1)
    return pl.pallas_call(
        functools.partial(_ffn_kernel, n_chunk=1024),
        grid=(r // tm, f // tf),
        in_specs=[
            pl.BlockSpec((tm, d), lambda i, j: (i, 0), pipeline_mode=once),
            pl.BlockSpec((None, None, d, tf), lambda i, j: (layer, slot, 0, j)),
            pl.BlockSpec((None, None, d, tf), lambda i, j: (layer, slot, 0, j)),
            pl.BlockSpec((None, None, tf, d), lambda i, j: (layer, slot, j, 0)),
        ],
        out_specs=pl.BlockSpec((tm, d), lambda i, j: (i, 0), pipeline_mode=once),
        out_shape=jax.ShapeDtypeStruct((r, d), BF16),
        scratch_shapes=[pltpu.VMEM((tm, d), F32)],
        compiler_params=_params(("parallel", "arbitrary"), 60),
        name="ffn",
    )(h, wg, wu, wd)


ROPE_SLAB = 2 * HEAD_DIM


def _proj_kernel(*refs, tile_kinds):
    if tile_kinds is None:
        h_ref, w_ref, o_ref = refs
        o_ref[...] = jnp.dot(h_ref[...], w_ref[...].astype(BF16), preferred_element_type=F32).astype(o_ref.dtype)
        return
    h_ref, w_ref, cos_ref, sa_ref, sb_ref, o_ref = refs
    j = pl.program_id(1)

    for modes, tiles in tile_kinds:
        @pl.when(functools.reduce(jnp.logical_or, [j == t for t in tiles]))
        def _(modes=modes):
            if all(m is None for m in modes):
                o_ref[...] = jnp.dot(h_ref[...], w_ref[...].astype(BF16),
                                     preferred_element_type=F32).astype(o_ref.dtype)
                return
            tabs = {"k": (cos_ref[...], sa_ref[...], sb_ref[...])}
            if "q" in modes:
                tabs["q"] = tuple(t * (LOG2_E * HEAD_DIM ** -0.5) for t in tabs["k"])
            h = h_ref[...]
            for s, mode in enumerate(modes):
                c0 = s * ROPE_SLAB
                res = jnp.dot(h, w_ref[:, c0:c0 + ROPE_SLAB].astype(BF16), preferred_element_type=F32)
                if mode is None:
                    o_ref[:, c0:c0 + ROPE_SLAB] = res.astype(o_ref.dtype)
                    continue
                cos, sa, sb = tabs[mode]
                for k in range(ROPE_SLAB // HEAD_DIM):
                    t = res[:, k * HEAD_DIM:(k + 1) * HEAD_DIM]
                    o = t * cos + pltpu.roll(t, ROPE_PAIRS, 1) * sa + pltpu.roll(t, HEAD_DIM - ROPE_PAIRS, 1) * sb
                    o_ref[:, c0 + k * HEAD_DIM:c0 + (k + 1) * HEAD_DIM] = o.astype(o_ref.dtype)


def _proj(h, w, layer, out_dtype, *, col0=0, n_cols=None, tm=1024, tn=1024, rope=None):
    r, kdim = h.shape
    n_cols = w.shape[2] - col0 if n_cols is None else n_cols
    tm = min(tm, r)
    tn = min(tn, n_cols)
    jo = col0 // tn
    in_specs = [
        pl.BlockSpec((tm, kdim), lambda i, j: (i, 0)),
        pl.BlockSpec((None, kdim, tn), lambda i, j: (layer, 0, j + jo)),
    ]
    args = [h, w]
    tile_kinds = None
    if rope is not None:
        tables, seq_len, q_cols, k_cols = rope
        per_seq = seq_len // tm
        tab = pl.BlockSpec((tm, HEAD_DIM), lambda i, j: (i % per_seq, 0))
        in_specs += [tab, tab, tab]
        args += list(tables)

        def mode(c):
            return "q" if q_cols[0] <= c < q_cols[1] else "k" if k_cols[0] <= c < k_cols[1] else None

        kinds = {}
        for t in range(n_cols // tn):
            modes = tuple(mode(col0 + t * tn + c) for c in range(0, tn, ROPE_SLAB))
            kinds.setdefault(modes, []).append(t)
        tile_kinds = tuple((m, tuple(ts)) for m, ts in kinds.items())
    return pl.pallas_call(
        functools.partial(_proj_kernel, tile_kinds=tile_kinds),
        grid=(r // tm, n_cols // tn),
        in_specs=in_specs,
        out_specs=pl.BlockSpec((tm, tn), lambda i, j: (i, j)),
        out_shape=jax.ShapeDtypeStruct((r, n_cols), out_dtype),
        compiler_params=_params(("parallel", "arbitrary"), 52),
        name="proj",
    )(*args)


def _rope_tables(n):
    row = (jnp.arange(n) // GRID_W).astype(F32)
    col = (jnp.arange(n) % GRID_W).astype(F32)
    inv = ROPE_BASE ** (-jnp.arange(ROPE_PAIRS, dtype=F32) / ROPE_PAIRS)
    ar = row[:, None] * inv
    ac = col[:, None] * inv
    ang = jnp.concatenate([ar, ar, ac, ac], axis=-1)
    cos, sin = jnp.cos(ang), jnp.sin(ang)
    upper = (jnp.arange(HEAD_DIM) % (2 * ROPE_PAIRS)) >= ROPE_PAIRS
    return cos, jnp.where(upper, sin, 0.0), jnp.where(upper, 0.0, -sin)


def _conv_kernel(x_ref, w_ref, b_ref, o_ref, pad_ref, *, rows):
    n = x_ref.shape[1]
    tc = x_ref.shape[2]
    halo = CONV_K // 2
    zeros = jnp.zeros((SUBLANES, tc), F32)
    pad_ref[0:SUBLANES, :] = zeros
    pad_ref[n + SUBLANES:n + 2 * SUBLANES, :] = zeros
    pad_ref[SUBLANES:n + SUBLANES, :] = x_ref[0].astype(F32)
    w = w_ref[...]
    bias = b_ref[...]
    for r0 in range(0, n, rows):
        acc = jnp.broadcast_to(bias, (rows, tc))
        for k in range(CONV_K):
            start = SUBLANES + r0 + k - halo
            acc = acc + pad_ref[start:start + rows, :] * w[k:k + 1, :]
        o_ref[0, r0:r0 + rows, :] = _silu(acc).astype(o_ref.dtype)


def _conv_silu(u, w, bias, tc=128):
    b, n, c = u.shape
    rows = min(512, n)
    return pl.pallas_call(
        functools.partial(_conv_kernel, rows=rows),
        grid=(b, c // tc),
        in_specs=[
            pl.BlockSpec((1, n, tc), lambda i, j: (i, 0, j)),
            pl.BlockSpec((CONV_K, tc), lambda i, j: (0, j)),
            pl.BlockSpec((1, tc), lambda i, j: (0, j)),
        ],
        out_specs=pl.BlockSpec((1, n, tc), lambda i, j: (i, 0, j)),
        out_shape=jax.ShapeDtypeStruct((b, n, c), BF16),
        scratch_shapes=[pltpu.VMEM((n + 2 * SUBLANES, tc), F32)],
        compiler_params=_params(("parallel", "parallel"), 40),
        name="conv_silu",
    )(u, w, bias.reshape(1, c))


def _attn_kernel(sink_ref, q_ref, kp_ref, ko_ref, kn_ref, vp_ref, vo_ref, vn_ref, kc_ref, vc_ref, o_ref):
    n = pl.program_id(1)
    nb = pl.num_programs(1)
    rows = GQA_GROUP * BLOCK
    qi = lax.broadcasted_iota(jnp.int32, (rows, BLOCK), 0) & (BLOCK - 1)
    kj = lax.broadcasted_iota(jnp.int32, (rows, BLOCK), 1)
    prev_ok = kj >= qi + jnp.where(n > 0, 0, BLOCK)
    next_ok = kj <= qi - jnp.where(n < nb - 1, 0, BLOCK)
    for kv in range(N_KV_HEADS):
        heads = range(kv * GQA_GROUP, (kv + 1) * GQA_GROUP)
        cols = slice(kv * HEAD_DIM, (kv + 1) * HEAD_DIM)
        qs = jnp.concatenate([q_ref[0, :, h * HEAD_DIM:(h + 1) * HEAD_DIM] for h in heads], axis=0)
        kb = jnp.concatenate([kp_ref[0, :, cols], ko_ref[0, :, cols], kn_ref[0, :, cols], kc_ref[0, :, cols]], axis=0)
        vb = jnp.concatenate([vp_ref[0, :, cols], vo_ref[0, :, cols], vn_ref[0, :, cols], vc_ref[0, :, cols]], axis=0)
        s = lax.dot_general(qs, kb, (((1,), (1,)), ((), ())), preferred_element_type=F32)
        s = jnp.concatenate([
            jnp.where(prev_ok, s[:, 0:BLOCK], NEG_INF), s[:, BLOCK:2 * BLOCK],
            jnp.where(next_ok, s[:, 2 * BLOCK:3 * BLOCK], NEG_INF), s[:, 3 * BLOCK:]], axis=1)
        sink = jnp.concatenate([jnp.full((BLOCK, 1), sink_ref[h] * LOG2_E, F32) for h in heads], axis=0)
        m = jnp.maximum(jnp.max(s, axis=-1, keepdims=True), sink)
        p = jnp.exp2(s - m)
        denom = jnp.sum(p, axis=-1, keepdims=True) + jnp.exp2(sink - m)
        o = jnp.dot(p.astype(BF16), vb, preferred_element_type=F32) / denom
        for g, h in enumerate(heads):
            o_ref[0, :, h * HEAD_DIM:(h + 1) * HEAD_DIM] = o[g * BLOCK:(g + 1) * BLOCK].astype(o_ref.dtype)


def _attention(a, kvc, sink, q_width, k_col0, v_col0):
    b, n, _ = a.shape
    lc = kvc.shape[1]
    nb = n // BLOCK
    kw = N_KV_HEADS * HEAD_DIM
    kb0, vb0 = k_col0 // kw, v_col0 // kw

    def kv_spec(col_blk, shift):
        return pl.BlockSpec((1, BLOCK, kw), lambda i, j: (i, jnp.clip(j + shift, 0, nb - 1), col_blk))

    return pl.pallas_call(
        _attn_kernel,
        grid=(b, nb),
        in_specs=[
            pl.BlockSpec(memory_space=pltpu.SMEM),
            pl.BlockSpec((1, BLOCK, q_width), lambda i, j: (i, j, 0)),
            kv_spec(kb0, -1), kv_spec(kb0, 0), kv_spec(kb0, 1),
            kv_spec(vb0, -1), kv_spec(vb0, 0), kv_spec(vb0, 1),
            pl.BlockSpec((1, lc, kw), lambda i, j: (i, 0, 0)),
            pl.BlockSpec((1, lc, kw), lambda i, j: (i, 0, 1)),
        ],
        out_specs=pl.BlockSpec((1, BLOCK, q_width), lambda i, j: (i, j, 0)),
        out_shape=jax.ShapeDtypeStruct((b, n, q_width), BF16),
        compiler_params=_params(("parallel", "arbitrary"), 32),
        name="window_attn",
    )(sink, a, a, a, a, a, a, a, kvc, kvc)


def _split_bf16(t, parts):
    out = []
    for _ in range(parts):
        hi = t.astype(BF16)
        out.append(hi)
        t = t - hi.astype(F32)
    return out


def _ssd_kernel(*refs, n_ctx_chunks, heads, inner):
    (xlf, blf, clf, dlf, xlb, blb, clb, dlb,
     xcf, bcf, ccf, dcf, xcb, bcb, ccb, dcb,
     alog_ref, bias_ref, e_ref, yf_ref, yb_ref,
     x_s, b_s, c_s, dt_s, h_s) = refs
    step = pl.program_id(1)
    hpg = heads // SSM_GROUPS
    gw = hpg * SSM_HEAD_DIM
    is_ctx = step < n_ctx_chunks

    @pl.when(step == 0)
    def _():
        h_s[...] = jnp.zeros_like(h_s)

    @pl.when(is_ctx)
    def _():
        for d, (xr, br, cr, dr) in enumerate(((xcf, bcf, ccf, dcf), (xcb, bcb, ccb, dcb))):
            x_s[d], b_s[d], c_s[d], dt_s[d] = xr[0], br[0], cr[0], dr[0]

    @pl.when(jnp.logical_not(is_ctx))
    def _():
        for d, (xr, br, cr, dr) in enumerate(((xlf, blf, clf, dlf), (xlb, blb, clb, dlb))):
            x_s[d], b_s[d], c_s[d], dt_s[d] = xr[0], br[0], cr[0], dr[0]

    lane = lax.broadcasted_iota(jnp.int32, (CHUNK, LANES), 1)
    row = lax.broadcasted_iota(jnp.int32, (CHUNK, CHUNK), 0)
    col = lax.broadcasted_iota(jnp.int32, (CHUNK, CHUNK), 1)
    is_fwd = lane < heads
    live = lane < 2 * heads

    dt_raw = jnp.where(is_fwd, dt_s[0], dt_s[1])
    dtv = jax.nn.softplus(dt_raw + bias_ref[...])
    a_row = jnp.where(live[0:1], -jnp.exp(alog_ref[...]), 0.0)
    da = dtv * a_row
    tri = jnp.where(row >= col, 1.0, 0.0).astype(BF16)
    acs = sum(jnp.dot(tri, part, preferred_element_type=F32) for part in _split_bf16(da, 3))
    tot = acs[CHUNK - 1:CHUNK, :]
    u = acs - jnp.where(is_fwd, 0.0, da)
    e_u = jnp.exp(u)
    e_r = jnp.exp(tot - u)
    rs = jnp.where(is_fwd, e_u, e_r)
    wst = jnp.where(is_fwd, e_r, e_u) * dtv
    packed = jnp.where(lane < 2 * heads, rs, pltpu.roll(wst, 2 * heads, 1))
    hi, lo = _split_bf16(packed, 2)
    packed2 = jnp.concatenate([hi, lo], axis=1)

    def expand(k):
        return jnp.dot(packed2, e_ref[:, k * inner:(k + 1) * inner], preferred_element_type=F32)

    us2 = jnp.where(is_fwd, u, -u) * LOG2_E
    col_t = (us2 - jnp.log2(dtv)).T

    for d in range(2):
        rs_x = expand(d)
        w_x = expand(2 + d)
        dec = rs_x[CHUNK - 1:CHUNK, :] if d == 0 else rs_x[0:1, :]
        keep = (row >= col) if d == 0 else (col >= row)
        y_ref = yf_ref if d == 0 else yb_ref
        for g in range(SSM_GROUPS):
            bm = b_s[d, :, g * SSM_STATE:(g + 1) * SSM_STATE]
            cm = c_s[d, :, g * SSM_STATE:(g + 1) * SSM_STATE]
            xg = x_s[d, :, g * gw:(g + 1) * gw]
            cb = lax.dot_general(cm, bm, (((1,), (1,)), ((), ())), preferred_element_type=F32)
            h_old = h_s[d, :, g * gw:(g + 1) * gw]
            y_off = jnp.dot(cm, h_old.astype(BF16), preferred_element_type=F32) * rs_x[:, g * gw:(g + 1) * gw]
            pieces = []
            for r in range(0, hpg, 2):
                mats = []
                for hh in (g * hpg + r, g * hpg + r + 1):
                    ln = hh + d * heads
                    diff = us2[:, ln:ln + 1] - col_t[ln:ln + 1, :]
                    decay_dt = jnp.exp2(jnp.where(keep, diff, -jnp.inf))
                    mats.append((cb * decay_dt).astype(BF16))
                xp = xg[:, r * SSM_HEAD_DIM:(r + 2) * SSM_HEAD_DIM]
                left = lane < SSM_HEAD_DIM
                zero = jnp.zeros_like(xp)
                rhs = jnp.concatenate([jnp.where(left, xp, zero), jnp.where(left, zero, xp)], axis=0)
                pieces.append(jnp.dot(jnp.concatenate(mats, axis=1), rhs, preferred_element_type=F32))
            y_g = jnp.concatenate(pieces, axis=1) + y_off
            y_ref[0, :, g * gw:(g + 1) * gw] = y_g.astype(y_ref.dtype)

            xw = (xg.astype(F32) * w_x[:, g * gw:(g + 1) * gw]).astype(BF16)
            st = lax.dot_general(bm, xw, (((0,), (0,)), ((), ())), preferred_element_type=F32)
            h_s[d, :, g * gw:(g + 1) * gw] = h_old * dec[:, g * gw:(g + 1) * gw] + st


def _ssd(xbc_lat, dt_lat, xbc_ctx, dt_ctx, a_log, dt_bias, inner, heads):
    b, n, _ = xbc_lat.shape
    lc = xbc_ctx.shape[1]
    nlat, ncc = n // CHUNK, lc // CHUNK
    bc = SSM_GROUPS * SSM_STATE
    assert 4 * heads == LANES and inner % bc == 0
    b_blk, c_blk = inner // bc, inner // bc + 1

    def lat_f(s):
        return jnp.maximum(s - ncc, 0)

    def lat_b(s):
        return nlat - 1 - jnp.maximum(s - ncc, 0)

    def ctx_f(s):
        return jnp.minimum(s, ncc - 1)

    def ctx_b(s):
        return jnp.maximum(ncc - 1 - s, 0)

    def specs(chunk_of):
        return [
            pl.BlockSpec((1, CHUNK, inner), lambda i, s: (i, chunk_of(s), 0)),
            pl.BlockSpec((1, CHUNK, bc), lambda i, s: (i, chunk_of(s), b_blk)),
            pl.BlockSpec((1, CHUNK, bc), lambda i, s: (i, chunk_of(s), c_blk)),
        ]

    def dt_spec(chunk_of):
        return [pl.BlockSpec((1, CHUNK, LANES), lambda i, s: (i, chunk_of(s), 0))]

    in_specs = (specs(lat_f) + dt_spec(lat_f) + specs(lat_b) + dt_spec(lat_b)
                + specs(ctx_f) + dt_spec(ctx_f) + specs(ctx_b) + dt_spec(ctx_b))
    const = lambda shape: pl.BlockSpec(shape, lambda i, s: (0, 0))
    in_specs += [const((1, LANES)), const((1, LANES)), const((2 * LANES, 4 * inner))]

    pad = jnp.zeros((LANES - 2 * heads,), F32)
    alog_row = jnp.concatenate([a_log.reshape(-1).astype(F32), pad]).reshape(1, LANES)
    bias_row = jnp.concatenate([dt_bias.reshape(-1).astype(F32), pad]).reshape(1, LANES)
    cidx = jnp.arange(4 * inner)
    src = heads * (cidx // inner) + (cidx % inner) // SSM_HEAD_DIM
    expand = (jnp.arange(LANES)[:, None] == src[None, :]).astype(BF16)
    expand = jnp.concatenate([expand, expand], axis=0)

    lat = [xbc_lat, xbc_lat, xbc_lat, dt_lat]
    ctx = [xbc_ctx, xbc_ctx, xbc_ctx, dt_ctx]
    y_shape = jax.ShapeDtypeStruct((b, n, inner), BF16)
    return pl.pallas_call(
        functools.partial(_ssd_kernel, n_ctx_chunks=ncc, heads=heads, inner=inner),
        grid=(b, ncc + nlat),
        in_specs=in_specs,
        out_specs=[
            pl.BlockSpec((1, CHUNK, inner), lambda i, s: (i, lat_f(s), 0)),
            pl.BlockSpec((1, CHUNK, inner), lambda i, s: (i, lat_b(s), 0)),
        ],
        out_shape=[y_shape, y_shape],
        scratch_shapes=[
            pltpu.VMEM((2, CHUNK, inner), BF16),
            pltpu.VMEM((2, CHUNK, bc), BF16),
            pltpu.VMEM((2, CHUNK, bc), BF16),
            pltpu.VMEM((2, CHUNK, LANES), F32),
            pltpu.VMEM((2, SSM_STATE, inner), F32),
        ],
        compiler_params=_params(("parallel", "arbitrary"), 48),
        name="ssd_scan",
    )(*lat, *lat, *ctx, *ctx, alog_row, bias_row, expand)


def _merge_kernel(attn_ref, yf_ref, yb_ref, xs_ref, z_ref, d_ref, an_ref, sn_ref, o_ref):
    aw = attn_ref.shape[2]
    a = attn_ref[0].astype(F32)
    o_ref[0, :, 0:aw] = (a * _rms_scale(a) * an_ref[...]).astype(BF16)
    y = yf_ref[0].astype(F32) + yb_ref[0].astype(F32) + d_ref[...] * xs_ref[0].astype(F32)
    t = y * _silu(z_ref[0].astype(F32))
    o_ref[0, :, aw:] = (t * _rms_scale(t) * sn_ref[...]).astype(BF16)


def _merge(attn, y_f, y_b, xbc, a, z_col0, d_exp, attn_norm, ssm_norm, tr=256):
    b, n, aw = attn.shape
    inner = y_f.shape[2]
    row = lambda w, cb=0: pl.BlockSpec((1, tr, w), lambda i, r: (i, r, cb))
    par = lambda w: pl.BlockSpec((1, w), lambda i, r: (0, 0))
    return pl.pallas_call(
        _merge_kernel,
        grid=(b, n // tr),
        in_specs=[row(aw), row(inner), row(inner), row(inner), row(inner, z_col0 // inner),
                  par(inner), par(aw), par(inner)],
        out_specs=row(aw + inner),
        out_shape=jax.ShapeDtypeStruct((b, n, aw + inner), BF16),
        compiler_params=_params(("parallel", "parallel"), 40),
        name="merge",
    )(attn, y_f, y_b, xbc, a, d_exp.reshape(1, inner), attn_norm.reshape(1, aw), ssm_norm.reshape(1, inner))


def kernel(x, c, ctx, c_ctx, w_mod, b_mod, norm_pre, norm_post, w_ffn_gate, w_ffn_up, w_ffn_down,
           w_in, attn_sink, attn_norm, conv_w, conv_b, a_log, dt_bias, d_skip, ssm_norm, w_out):
    bsz, n, d = x.shape
    lc = ctx.shape[1]
    depth = w_mod.shape[0]
    attn_w = N_HEADS * HEAD_DIM
    inner = d // 2
    heads = inner // SSM_HEAD_DIM
    kv_w = N_KV_HEADS * HEAD_DIM
    bc_w = SSM_GROUPS * SSM_STATE
    xbc_w = inner + 2 * bc_w
    ctx_col0 = attn_w + inner
    xbc_col0 = ctx_col0 + 2 * kv_w
    dt_col0 = xbc_col0 + xbc_w
    rope_tabs = _rope_tables(n)
    wg_all, wu_all, wd_all = w_ffn_gate, w_ffn_up, w_ffn_down
    w_in_all, w_out_all = w_in.astype(BF16), w_out

    for layer in range(depth):
        last = layer == depth - 1
        cv = jnp.zeros((SUBLANES, d), F32).at[:bsz].set(c).at[bsz].set(c_ctx)
        mod = _adaln(cv, w_mod[layer], b_mod[layer]).reshape(SUBLANES, N_MOD, d)
        mod_x = mod[:bsz]
        mod_c = jnp.broadcast_to(mod[bsz:bsz + 1], (bsz, N_MOD, d))

        ffn = functools.partial(_ffn, wg=wg_all, wu=wu_all, wd=wd_all, layer=layer)

        h = _prenorm(x, norm_pre[layer, 0], mod_x[:, 0], mod_x[:, 1])
        y = ffn(h.reshape(bsz * n, d), slot=0)
        x, hx = _resid(x, y, norm_post[layer, 0], mod_x[:, 2], 0.5,
                       nxt=(norm_pre[layer, 1], mod_x[:, 3], mod_x[:, 4]))
        hcx = _prenorm(ctx, norm_pre[layer, 0], mod_c[:, 0], mod_c[:, 1])
        yc = ffn(hcx.reshape(bsz * lc, d), slot=0)
        ctx, hc = _resid(ctx, yc, norm_post[layer, 0], mod_c[:, 2], 0.5,
                         nxt=(norm_pre[layer, 1], mod_c[:, 3], mod_c[:, 4]))

        w_dt = jnp.pad(w_in_all[layer:layer + 1, :, dt_col0:], ((0, 0), (0, 0), (0, LANES - 2 * heads)))
        hx2, hc2 = hx.reshape(bsz * n, d), hc.reshape(bsz * lc, d)
        a_lat = _proj(hx2, w_in_all, layer, BF16, n_cols=xbc_col0,
                      rope=(rope_tabs, n, (0, attn_w), (ctx_col0, ctx_col0 + kv_w))).reshape(bsz, n, xbc_col0)
        xbc_lat = _proj(hx2, w_in_all, layer, BF16, col0=xbc_col0, n_cols=xbc_w).reshape(bsz, n, xbc_w)
        dt_lat = _proj(hx2, w_dt, 0, F32).reshape(bsz, n, LANES)
        kv_ctx = _proj(hc2, w_in_all, layer, BF16, col0=ctx_col0, n_cols=2 * kv_w).reshape(bsz, lc, 2 * kv_w)
        xbc_ctx = _proj(hc2, w_in_all, layer, BF16, col0=xbc_col0, n_cols=xbc_w).reshape(bsz, lc, xbc_w)
        dt_ctx = _proj(hc2, w_dt, 0, F32).reshape(bsz, lc, LANES)

        attn = _attention(a_lat, kv_ctx, attn_sink[layer].astype(F32), attn_w, ctx_col0, ctx_col0 + kv_w)
        xbc_lat = _conv_silu(xbc_lat, conv_w[layer], conv_b[layer])
        xbc_ctx = _conv_silu(xbc_ctx, conv_w[layer], conv_b[layer])
        y_f, y_b = _ssd(xbc_lat, dt_lat, xbc_ctx, dt_ctx, a_log[layer], dt_bias[layer], inner, heads)
        d_exp = jnp.repeat(d_skip[layer].astype(F32), SSM_HEAD_DIM)
        merged = _merge(attn, y_f, y_b, xbc_lat, a_lat, attn_w, d_exp, attn_norm[layer], ssm_norm[layer])
        y = _proj(merged.reshape(bsz * n, attn_w + inner), w_out_all, layer, BF16, tn=512)
        x, h = _resid(x, y, norm_post[layer, 1], mod_x[:, 5], 1.0,
                      nxt=(norm_pre[layer, 2], mod_x[:, 6], mod_x[:, 7]))

        y = ffn(h.reshape(bsz * n, d), slot=1)
        x = _resid(x, y, norm_post[layer, 2], mod_x[:, 8], 0.5)
        if not last:
            raise NotImplementedError("context output path is only needed for depth > 1")
    return x
```

```python
import functools

import jax
import jax.numpy as jnp
from jax import lax
from jax.experimental import pallas as pl
from jax.experimental.pallas import tpu as pltpu

GRID_W = 64
HEAD_DIM = 128
N_HEADS = 16
N_KV_HEADS = 4
GQA_GROUP = N_HEADS // N_KV_HEADS
BLOCK = 128
ROPE_BASE = 10000.0
ROPE_PAIRS = HEAD_DIM // 4
SSM_HEAD_DIM = 64
SSM_GROUPS = 4
SSM_STATE = 128
CONV_K = 5
CHUNK = 128
N_MOD = 9
EPS = 1e-6
NEG_INF = -1e30
LOG2_E = 1.4426950408889634

LANES = 128
SUBLANES = 8
MIB = 1024 * 1024

F32 = jnp.float32
BF16 = jnp.bfloat16


def _params(semantics, vmem_mib):
    return pltpu.CompilerParams(dimension_semantics=semantics, vmem_limit_bytes=vmem_mib * MIB)


def _stream_spec(block, index_map):
    return pl.BlockSpec(block, index_map)


def _silu(t):
    return t * jax.nn.sigmoid(t)


def _rms_scale(t):
    return lax.rsqrt(jnp.mean(t * t, axis=-1, keepdims=True) + EPS)


def _mod_kernel(c_ref, w_ref, b_ref, o_ref):
    s = _silu(c_ref[...]).astype(BF16)
    o_ref[...] = jnp.dot(s, w_ref[...].astype(BF16), preferred_element_type=F32) + b_ref[...]


def _adaln(cv, w_mod, b_mod):
    rows, d = cv.shape
    n = w_mod.shape[1]
    tn = 1024
    return pl.pallas_call(
        _mod_kernel,
        grid=(n // tn,),
        in_specs=[
            pl.BlockSpec((rows, d), lambda j: (0, 0)),
            _stream_spec((d, tn), lambda j: (0, j)),
            pl.BlockSpec((1, tn), lambda j: (0, j)),
        ],
        out_specs=pl.BlockSpec((rows, tn), lambda j: (0, j)),
        out_shape=jax.ShapeDtypeStruct((rows, n), F32),
        compiler_params=_params(("parallel",), 48),
        name="adaln",
    )(cv, w_mod, b_mod.reshape(1, n))


def _prenorm_kernel(s_ref, g_ref, sh_ref, sc_ref, h_ref):
    s = s_ref[0]
    h = s * _rms_scale(s) * g_ref[...]
    h_ref[0] = (h * (1.0 + sc_ref[0]) + sh_ref[0]).astype(BF16)


def _prenorm(s, g_pre, shift, scale, tr=512):
    b, n, d = s.shape
    tr = min(tr, n)
    row_map = lambda i, r: (i, r, 0)
    vec = pl.BlockSpec((1, 1, d), lambda i, r: (i, 0, 0))
    return pl.pallas_call(
        _prenorm_kernel,
        grid=(b, n // tr),
        in_specs=[_stream_spec((1, tr, d), row_map), pl.BlockSpec((1, d), lambda i, r: (0, 0)), vec, vec],
        out_specs=pl.BlockSpec((1, tr, d), row_map),
        out_shape=jax.ShapeDtypeStruct((b, n, d), BF16),
        compiler_params=_params(("parallel", "parallel"), 40),
        name="prenorm",
    )(s, g_pre.reshape(1, d), shift.reshape(b, 1, d), scale.reshape(b, 1, d))


def _resid_kernel(*refs, coef, with_h):
    if with_h:
        s_ref, y_ref, gp_ref, gate_ref, gn_ref, sh_ref, sc_ref, o_ref, h_ref = refs
    else:
        s_ref, y_ref, gp_ref, gate_ref, o_ref = refs
    y = y_ref[0].astype(F32)
    s_new =s_ref[0] + (coef * gate_ref[0]) * (y * _rms_scale(y) * gp_ref[...])
    o_ref[0] = s_new
    if with_h:
        h = s_new * _rms_scale(s_new) * gn_ref[...]
        h_ref[0] = (h * (1.0 + sc_ref[0]) + sh_ref[0]).astype(BF16)


def _resid(s, y, g_post, gate, coef, nxt=None):
    b, n, d = s.shape
    tr = min(n, 256 if nxt is not None else 512)
    row_map = lambda i, r: (i, r, 0)
    row = pl.BlockSpec((1, tr, d), row_map)
    vec = pl.BlockSpec((1, 1, d), lambda i, r: (i, 0, 0))
    par = pl.BlockSpec((1, d), lambda i, r: (0, 0))
    args = [s, y.reshape(b, n, d), g_post.reshape(1, d), gate.reshape(b, 1, d)]
    in_specs = [_stream_spec((1, tr, d), row_map), _stream_spec((1, tr, d), row_map), par, vec]
    out_shape = [jax.ShapeDtypeStruct((b, n, d), F32)]
    out_specs = [row]
    if nxt is not None:
        g_next, shift, scale = nxt
        args += [g_next.reshape(1, d), shift.reshape(b, 1, d), scale.reshape(b, 1, d)]
        in_specs += [par, vec, vec]
        out_shape.append(jax.ShapeDtypeStruct((b, n, d), BF16))
        out_specs.append(row)
    out = pl.pallas_call(
        functools.partial(_resid_kernel, coef=coef, with_h=nxt is not None),
        grid=(b, n // tr),
        in_specs=in_specs,
        out_specs=out_specs,
        out_shape=out_shape,
        compiler_params=_params(("parallel", "parallel"), 56),
        name="resid",
    )(*args)
    return out if nxt is not None else out[0]


def _ffn_kernel(h_ref, wg_ref, wu_ref, wd_ref, y_ref, acc_ref, *, n_chunk):
    j = pl.program_id(1)

    @pl.when(j == 0)
    def _():
        acc_ref[...] = jnp.zeros_like(acc_ref)

    h = h_ref[...]
    g = jnp.dot(h, wg_ref[...].astype(BF16), preferred_element_type=F32)
    u = jnp.dot(h, wu_ref[...].astype(BF16), preferred_element_type=F32)
    a = (_silu(g) * u).astype(BF16)
    d = acc_ref.shape[1]
    for n0 in range(0, d, n_chunk):
        wd = wd_ref[:, n0:n0 + n_chunk].astype(BF16)
        acc_ref[:, n0:n0 + n_chunk] += jnp.dot(a, wd, preferred_element_type=F32)

    @pl.when(j == pl.num_programs(1) - 1)
    def _():
        y_ref[...] = acc_ref[...].astype(y_ref.dtype)


def _ffn(h, wg, wu, wd, layer, slot, tm=1024, tf=256):
    r, d = h.shape
    f = wg.shape[3]
    tm = min(tm, r)
    once = pl.Buffered(1)
    return pl.pallas_call(
        functools.partial(_ffn_kernel, n_chunk=1024),
        grid=(r // tm, f // tf),
        in_specs=[
            pl.BlockSpec((tm, d), lambda i, j: (i, 0), pipeline_mode=once),
            pl.BlockSpec((None, None, d, tf), lambda i, j: (layer, slot, 0, j)),
            pl.BlockSpec((None, None, d, tf), lambda i, j: (layer, slot, 0, j)),
            pl.BlockSpec((None, None, tf, d), lambda i, j: (layer, slot, j, 0)),
        ],
        out_specs=pl.BlockSpec((tm, d), lambda i, j: (i, 0), pipeline_mode=once),
        out_shape=jax.ShapeDtypeStruct((r, d), BF16),
        scratch_shapes=[pltpu.VMEM((tm, d), F32)],
        compiler_params=_params(("parallel", "arbitrary"), 60),
        name="ffn",
    )(h, wg, wu, wd)


ROPE_SLAB = 2 * HEAD_DIM


def _proj_kernel(*refs, tile_kinds):
    if tile_kinds is None:
        h_ref, w_ref, o_ref = refs
        o_ref[...] = jnp.dot(h_ref[...], w_ref[...].astype(BF16), preferred_element_type=F32).astype(o_ref.dtype)
        return
    h_ref, w_ref, cos_ref, sa_ref, sb_ref, o_ref = refs
    j = pl.program_id(1)

    for modes, tiles in tile_kinds:
        @pl.when(functools.reduce(jnp.logical_or, [j == t for t in tiles]))
        def _(modes=modes):
            if all(m is None for m in modes):
                o_ref[...] = jnp.dot(h_ref[...], w_ref[...].astype(BF16),
                                     preferred_element_type=F32).astype(o_ref.dtype)
                return
            tabs = {"k": (cos_ref[...], sa_ref[...], sb_ref[...])}
            if "q" in modes:
                tabs["q"] = tuple(t * (LOG2_E * HEAD_DIM ** -0.5) for t in tabs["k"])
            h = h_ref[...]
            for s, mode in enumerate(modes):
                c0 = s * ROPE_SLAB
                res = jnp.dot(h, w_ref[:, c0:c0 + ROPE_SLAB].astype(BF16), preferred_element_type=F32)
                if mode is None:
                    o_ref[:, c0:c0 + ROPE_SLAB] = res.astype(o_ref.dtype)
                    continue
                cos, sa, sb = tabs[mode]
                for k in range(ROPE_SLAB // HEAD_DIM):
                    t = res[:, k * HEAD_DIM:(k + 1) * HEAD_DIM]
                    o = t * cos + pltpu.roll(t, ROPE_PAIRS, 1) * sa + pltpu.roll(t, HEAD_DIM - ROPE_PAIRS, 1) * sb
                    o_ref[:, c0 + k * HEAD_DIM:c0 + (k + 1) * HEAD_DIM] = o.astype(o_ref.dtype)


def _proj(h, w, layer, out_dtype, *, col0=0, n_cols=None, tm=1024, tn=1024, rope=None):
    r, kdim = h.shape
    n_cols = w.shape[2] - col0 if n_cols is None else n_cols
    tm = min(tm, r)
    tn = min(tn, n_cols)
    jo = col0 // tn
    in_specs = [
        pl.BlockSpec((tm, kdim), lambda i, j: (i, 0)),
        pl.BlockSpec((None, kdim, tn), lambda i, j: (layer, 0, j + jo)),
    ]
    args = [h, w]
    tile_kinds = None
    if rope is not None:
        tables, seq_len, q_cols, k_cols = rope
        per_seq = seq_len // tm
        tab = pl.BlockSpec((tm, HEAD_DIM), lambda i, j: (i % per_seq, 0))
        in_specs += [tab, tab, tab]
        args += list(tables)

        def mode(c):
            return "q" if q_cols[0] <= c < q_cols[1] else "k" if k_cols[0] <= c < k_cols[1] else None

        kinds = {}
        for t in range(n_cols // tn):
            modes = tuple(mode(col0 + t * tn + c) for c in range(0, tn, ROPE_SLAB))
            kinds.setdefault(modes, []).append(t)
        tile_kinds = tuple((m, tuple(ts)) for m, ts in kinds.items())
    return pl.pallas_call(
        functools.partial(_proj_kernel, tile_kinds=tile_kinds),
        grid=(r // tm, n_cols // tn),
        in_specs=in_specs,
        out_specs=pl.BlockSpec((tm, tn), lambda i, j: (i, j)),
        out_shape=jax.ShapeDtypeStruct((r, n_cols), out_dtype),
        compiler_params=_params(("parallel", "arbitrary"), 52),
        name="proj",
    )(*args)


def _rope_tables(n):
    row = (jnp.arange(n) // GRID_W).astype(F32)
    col = (jnp.arange(n) % GRID_W).astype(F32)
    inv = ROPE_BASE ** (-jnp.arange(ROPE_PAIRS, dtype=F32) / ROPE_PAIRS)
    ar = row[:, None] * inv
    ac = col[:, None] * inv
    ang = jnp.concatenate([ar, ar, ac, ac], axis=-1)
    cos, sin = jnp.cos(ang), jnp.sin(ang)
    upper = (jnp.arange(HEAD_DIM) % (2 * ROPE_PAIRS)) >= ROPE_PAIRS
    return cos, jnp.where(upper, sin, 0.0), jnp.where(upper, 0.0, -sin)


def _conv_kernel(x_ref, w_ref, b_ref, o_ref, pad_ref, *, rows):
    n = x_ref.shape[1]
    tc = x_ref.shape[2]
    halo = CONV_K // 2
    zeros = jnp.zeros((SUBLANES, tc), F32)
    pad_ref[0:SUBLANES, :] = zeros
    pad_ref[n + SUBLANES:n + 2 * SUBLANES, :] = zeros
    pad_ref[SUBLANES:n + SUBLANES, :] = x_ref[0].astype(F32)
    w = w_ref[...]
    bias = b_ref[...]
    for r0 in range(0, n, rows):
        acc = jnp.broadcast_to(bias, (rows, tc))
        for k in range(CONV_K):
            start = SUBLANES + r0 + k - halo
            acc = acc + pad_ref[start:start + rows, :] * w[k:k + 1, :]
        o_ref[0, r0:r0 + rows, :] = _silu(acc).astype(o_ref.dtype)


def _conv_silu(u, w, bias, tc=128):
    b, n, c = u.shape
    rows = min(512, n)
    return pl.pallas_call(
        functools.partial(_conv_kernel, rows=rows),
        grid=(b, c // tc),
        in_specs=[
            pl.BlockSpec((1, n, tc), lambda i, j: (i, 0, j)),
            pl.BlockSpec((CONV_K, tc), lambda i, j: (0, j)),
            pl.BlockSpec((1, tc), lambda i, j: (0, j)),
        ],
        out_specs=pl.BlockSpec((1, n, tc), lambda i, j: (i, 0, j)),
        out_shape=jax.ShapeDtypeStruct((b, n, c), BF16),
        scratch_shapes=[pltpu.VMEM((n + 2 * SUBLANES, tc), F32)],
        compiler_params=_params(("parallel", "parallel"), 40),
        name="conv_silu",
    )(u, w, bias.reshape(1, c))


def _attn_kernel(sink_ref, q_ref, kp_ref, ko_ref, kn_ref, vp_ref, vo_ref, vn_ref, kc_ref, vc_ref, o_ref):
    n = pl.program_id(1)
    nb = pl.num_programs(1)
    rows = GQA_GROUP * BLOCK
    qi = lax.broadcasted_iota(jnp.int32, (rows, BLOCK), 0) & (BLOCK - 1)
    kj = lax.broadcasted_iota(jnp.int32, (rows, BLOCK), 1)
    prev_ok = kj >= qi + jnp.where(n > 0, 0, BLOCK)
    next_ok = kj <= qi - jnp.where(n < nb - 1, 0, BLOCK)
    for bi, kv in [(bi, kv) for bi in range(o_ref.shape[0]) for kv in range(N_KV_HEADS)]:
        heads = range(kv * GQA_GROUP, (kv + 1) * GQA_GROUP)
        cols = slice(kv * HEAD_DIM, (kv + 1) * HEAD_DIM)
        qs = jnp.concatenate([q_ref[bi, :, h * HEAD_DIM:(h + 1) * HEAD_DIM] for h in heads], axis=0)
        kb = jnp.concatenate([r[bi, :, cols] for r in (kp_ref, ko_ref, kn_ref, kc_ref)], axis=0)
        vb = jnp.concatenate([r[bi, :, cols] for r in (vp_ref, vo_ref, vn_ref, vc_ref)], axis=0)
        s = lax.dot_general(qs, kb, (((1,), (1,)), ((), ())), preferred_element_type=F32)
        s = jnp.concatenate([
            jnp.where(prev_ok, s[:, 0:BLOCK], NEG_INF), s[:, BLOCK:2 * BLOCK],
            jnp.where(next_ok, s[:, 2 * BLOCK:3 * BLOCK], NEG_INF), s[:, 3 * BLOCK:]], axis=1)
        sink = jnp.concatenate([jnp.full((BLOCK, 1), sink_ref[h] * LOG2_E, F32) for h in heads], axis=0)
        m = jnp.maximum(jnp.max(s, axis=-1, keepdims=True), sink)
        p = jnp.exp2(s - m)
        denom = jnp.sum(p, axis=-1, keepdims=True) + jnp.exp2(sink - m)
        o = jnp.dot(p.astype(BF16), vb, preferred_element_type=F32) / denom
        for g, h in enumerate(heads):
            o_ref[bi, :, h * HEAD_DIM:(h + 1) * HEAD_DIM] = o[g * BLOCK:(g + 1) * BLOCK].astype(o_ref.dtype)


def _attention(a, kvc, sink, q_width, k_col0, v_col0):
    b, n, _ = a.shape
    lc = kvc.shape[1]
    nb = n // BLOCK
    kw = N_KV_HEADS * HEAD_DIM
    kb0, vb0 = k_col0 // kw, v_col0 // kw

    bb = b

    def kv_spec(col_blk, shift):
        return pl.BlockSpec((bb, BLOCK, kw), lambda i, j: (i, jnp.clip(j + shift, 0, nb - 1), col_blk))

    return pl.pallas_call(
        _attn_kernel,
        grid=(b // bb, nb),
        in_specs=[
            pl.BlockSpec(memory_space=pltpu.SMEM),
            pl.BlockSpec((bb, BLOCK, q_width), lambda i, j: (i, j, 0)),
            kv_spec(kb0, -1), kv_spec(kb0, 0), kv_spec(kb0, 1),
            kv_spec(vb0, -1), kv_spec(vb0, 0), kv_spec(vb0, 1),
            pl.BlockSpec((bb, lc, kw), lambda i, j: (i, 0, 0)),
            pl.BlockSpec((bb, lc, kw), lambda i, j: (i, 0, 1)),
        ],
        out_specs=pl.BlockSpec((bb, BLOCK, q_width), lambda i, j: (i, j, 0)),
        out_shape=jax.ShapeDtypeStruct((b, n, q_width), BF16),
        compiler_params=_params(("parallel", "arbitrary"), 32),
        name="window_attn",
    )(sink, a, a, a, a, a, a, a, kvc, kvc)


def _split_bf16(t, parts):
    out = []
    for _ in range(parts):
        hi = t.astype(BF16)
        out.append(hi)
        t = t - hi.astype(F32)
    return out


def _ssd_kernel(*refs, n_ctx_chunks, heads, inner):
    ins, consts, outs, scratch = refs[:16], refs[16:19], refs[19:21], refs[21:]
    step = pl.program_id(1)

    @pl.when(step == 0)
    def _():
        scratch[-1][...] = jnp.zeros_like(scratch[-1])

    def stage(srcs):
        for bi in range(outs[0].shape[0]):
            for d in range(2):
                for k in range(4):
                    scratch[k][bi, d] = srcs[4 * d + k][bi]

    pl.when(step < n_ctx_chunks)(lambda: stage(ins[8:16]))
    pl.when(step >= n_ctx_chunks)(lambda: stage(ins[0:8]))

    for bi in range(outs[0].shape[0]):
        _ssd_step(*consts, *(r.at[bi] for r in outs), *(r.at[bi] for r in scratch), heads=heads, inner=inner)


def _ssd_step(alog_ref, bias_ref, e_ref, yf_ref, yb_ref, x_s, b_s, c_s, dt_s, h_s, *, heads, inner):
    hpg = heads // SSM_GROUPS
    gw = hpg * SSM_HEAD_DIM

    lane = lax.broadcasted_iota(jnp.int32, (CHUNK, LANES), 1)
    row = lax.broadcasted_iota(jnp.int32, (CHUNK, CHUNK), 0)
    col = lax.broadcasted_iota(jnp.int32, (CHUNK, CHUNK), 1)
    is_fwd = lane < heads
    live = lane < 2 * heads

    dt_raw = jnp.where(is_fwd, dt_s[0], dt_s[1])
    dtv = jax.nn.softplus(dt_raw + bias_ref[...])
    a_row = jnp.where(live[0:1], -jnp.exp(alog_ref[...]), 0.0)
    da = dtv * a_row
    tri = jnp.where(row >= col, 1.0, 0.0).astype(BF16)
    acs = sum(jnp.dot(tri, part, preferred_element_type=F32) for part in _split_bf16(da, 3))
    tot = acs[CHUNK - 1:CHUNK, :]
    u = acs - jnp.where(is_fwd, 0.0, da)
    e_u = jnp.exp(u)
    e_r = jnp.exp(tot - u)
    rs = jnp.where(is_fwd, e_u, e_r)
    wst = jnp.where(is_fwd, e_r, e_u) * dtv
    packed = jnp.where(lane < 2 * heads, rs, pltpu.roll(wst, 2 * heads, 1))
    hi, lo = _split_bf16(packed, 2)
    packed2 = jnp.concatenate([hi, lo], axis=1)

    def expand(k):
        return jnp.dot(packed2, e_ref[:, k * inner:(k + 1) * inner], preferred_element_type=F32)

    us2 = jnp.where(is_fwd, u, -u) * LOG2_E
    col_t = (us2 - jnp.log2(dtv)).T

    for d in range(2):
        rs_x = expand(d)
        w_x = expand(2 + d)
        dec = rs_x[CHUNK - 1:CHUNK, :] if d == 0 else rs_x[0:1, :]
        keep = (row >= col) if d == 0 else (col >= row)
        y_ref = yf_ref if d == 0 else yb_ref
        for g in range(SSM_GROUPS):
            bm = b_s[d, :, g * SSM_STATE:(g + 1) * SSM_STATE]
            cm = c_s[d, :, g * SSM_STATE:(g + 1) * SSM_STATE]
            xg = x_s[d, :, g * gw:(g + 1) * gw]
            cb = lax.dot_general(cm, bm, (((1,), (1,)), ((), ())), preferred_element_type=F32)
            h_old = h_s[d, :, g * gw:(g + 1) * gw]
            y_off = jnp.dot(cm, h_old.astype(BF16), preferred_element_type=F32) * rs_x[:, g * gw:(g + 1) * gw]
            pieces = []
            for r in range(0, hpg, 2):
                mats = []
                for hh in (g * hpg + r, g * hpg + r + 1):
                    ln = hh + d * heads
                    diff = us2[:, ln:ln + 1] - col_t[ln:ln + 1, :]
                    decay_dt = jnp.exp2(jnp.where(keep, diff, -jnp.inf))
                    mats.append((cb * decay_dt).astype(BF16))
                xp = xg[:, r * SSM_HEAD_DIM:(r + 2) * SSM_HEAD_DIM]
                left = lane < SSM_HEAD_DIM
                zero = jnp.zeros_like(xp)
                rhs = jnp.concatenate([jnp.where(left, xp, zero), jnp.where(left, zero, xp)], axis=0)
                pieces.append(jnp.dot(jnp.concatenate(mats, axis=1), rhs, preferred_element_type=F32))
            y_g = jnp.concatenate(pieces, axis=1) + y_off
            y_ref[:, g * gw:(g + 1) * gw] = y_g.astype(y_ref.dtype)

            xw = (xg.astype(F32) * w_x[:, g * gw:(g + 1) * gw]).astype(BF16)
            st = lax.dot_general(bm, xw, (((0,), (0,)), ((), ())), preferred_element_type=F32)
            h_s[d, :, g * gw:(g + 1) * gw] = h_old * dec[:, g * gw:(g + 1) * gw] + st


def _ssd(xbc_lat, dt_lat, xbc_ctx, dt_ctx, a_log, dt_bias, inner, heads):
    b, n, _ = xbc_lat.shape
    lc = xbc_ctx.shape[1]
    nlat, ncc = n // CHUNK, lc // CHUNK
    bc = SSM_GROUPS * SSM_STATE
    assert 4 * heads == LANES and inner % bc == 0
    b_blk, c_blk = inner // bc, inner // bc + 1

    def lat_f(s):
        return jnp.maximum(s - ncc, 0)

    def lat_b(s):
        return nlat - 1 - jnp.maximum(s - ncc, 0)

    def ctx_f(s):
        return jnp.minimum(s, ncc - 1)

    def ctx_b(s):
        return jnp.maximum(ncc - 1 - s, 0)

    bb = b

    def specs(chunk_of):
        return [
            pl.BlockSpec((bb, CHUNK, inner), lambda i, s: (i, chunk_of(s), 0)),
            pl.BlockSpec((bb, CHUNK, bc), lambda i, s: (i, chunk_of(s), b_blk)),
            pl.BlockSpec((bb, CHUNK, bc), lambda i, s: (i, chunk_of(s), c_blk)),
        ]

    def dt_spec(chunk_of):
        return [pl.BlockSpec((bb, CHUNK, LANES), lambda i, s: (i, chunk_of(s), 0))]

    in_specs = (specs(lat_f) + dt_spec(lat_f) + specs(lat_b) + dt_spec(lat_b)
                + specs(ctx_f) + dt_spec(ctx_f) + specs(ctx_b) + dt_spec(ctx_b))
    const = lambda shape: pl.BlockSpec(shape, lambda i, s: (0, 0))
    in_specs += [const((1, LANES)), const((1, LANES)), const((2 * LANES, 4 * inner))]

    pad = jnp.zeros((LANES - 2 * heads,), F32)
    alog_row = jnp.concatenate([a_log.reshape(-1).astype(F32), pad]).reshape(1, LANES)
    bias_row = jnp.concatenate([dt_bias.reshape(-1).astype(F32), pad]).reshape(1, LANES)
    cidx = jnp.arange(4 * inner)
    src = heads * (cidx // inner) + (cidx % inner) // SSM_HEAD_DIM
    expand = (jnp.arange(LANES)[:, None] == src[None, :]).astype(BF16)
    expand = jnp.concatenate([expand, expand], axis=0)

    lat = [xbc_lat, xbc_lat, xbc_lat, dt_lat]
    ctx = [xbc_ctx, xbc_ctx, xbc_ctx, dt_ctx]
    y_shape = jax.ShapeDtypeStruct((b, n, inner), BF16)
    return pl.pallas_call(
        functools.partial(_ssd_kernel, n_ctx_chunks=ncc, heads=heads, inner=inner),
        grid=(b // bb, ncc + nlat),
        in_specs=in_specs,
        out_specs=[
            pl.BlockSpec((bb, CHUNK, inner), lambda i, s: (i, lat_f(s), 0)),
            pl.BlockSpec((bb, CHUNK, inner), lambda i, s: (i, lat_b(s), 0)),
        ],
        out_shape=[y_shape, y_shape],
        scratch_shapes=[
            pltpu.VMEM((bb, 2, CHUNK, inner), BF16),
            pltpu.VMEM((bb, 2, CHUNK, bc), BF16),
            pltpu.VMEM((bb, 2, CHUNK, bc), BF16),
            pltpu.VMEM((bb, 2, CHUNK, LANES), F32),
            pltpu.VMEM((bb, 2, SSM_STATE, inner), F32),
        ],
        compiler_params=_params(("parallel", "arbitrary"), 48),
        name="ssd_scan",
    )(*lat, *lat, *ctx, *ctx, alog_row, bias_row, expand)


def _merge_kernel(attn_ref, yf_ref, yb_ref, xs_ref, z_ref, d_ref, an_ref, sn_ref, o_ref):
    aw = attn_ref.shape[2]
    a = attn_ref[0].astype(F32)
    o_ref[0, :, 0:aw] = (a * _rms_scale(a) * an_ref[...]).astype(BF16)
    y = yf_ref[0].astype(F32) + yb_ref[0].astype(F32) + d_ref[...] * xs_ref[0].astype(F32)
    t = y * _silu(z_ref[0].astype(F32))
    o_ref[0, :, aw:] = (t * _rms_scale(t) * sn_ref[...]).astype(BF16)


def _merge(attn, y_f, y_b, xbc, a, z_col0, d_exp, attn_norm, ssm_norm, tr=512):
    b, n, aw = attn.shape
    inner = y_f.shape[2]
    row = lambda w, cb=0: _stream_spec((1, tr, w), lambda i, r: (i, r, cb))
    par = lambda w: pl.BlockSpec((1, w), lambda i, r: (0, 0))
    return pl.pallas_call(
        _merge_kernel,
        grid=(b, n // tr),
        in_specs=[row(aw), row(inner), row(inner), row(inner), row(inner, z_col0 // inner),
                  par(inner), par(aw), par(inner)],
        out_specs=pl.BlockSpec((1, tr, aw + inner), lambda i, r: (i, r, 0)),
        out_shape=jax.ShapeDtypeStruct((b, n, aw + inner), BF16),
        compiler_params=_params(("parallel", "parallel"), 48),
        name="merge",
    )(attn, y_f, y_b, xbc, a, d_exp.reshape(1, inner), attn_norm.reshape(1, aw), ssm_norm.reshape(1, inner))


def kernel(x, c, ctx, c_ctx, w_mod, b_mod, norm_pre, norm_post, w_ffn_gate, w_ffn_up, w_ffn_down,
           w_in, attn_sink, attn_norm, conv_w, conv_b, a_log, dt_bias, d_skip, ssm_norm, w_out):
    bsz, n, d = x.shape
    lc = ctx.shape[1]
    depth = w_mod.shape[0]
    attn_w = N_HEADS * HEAD_DIM
    inner = d // 2
    heads = inner // SSM_HEAD_DIM
    kv_w = N_KV_HEADS * HEAD_DIM
    bc_w = SSM_GROUPS * SSM_STATE
    xbc_w = inner + 2 * bc_w
    ctx_col0 = attn_w + inner
    xbc_col0 = ctx_col0 + 2 * kv_w
    dt_col0 = xbc_col0 + xbc_w
    rope_tabs = _rope_tables(n)
    wg_all, wu_all, wd_all = w_ffn_gate, w_ffn_up, w_ffn_down
    w_in_all, w_out_all = w_in.astype(BF16), w_out

    for layer in range(depth):
        last = layer == depth - 1
        cv = jnp.zeros((SUBLANES, d), F32).at[:bsz].set(c).at[bsz].set(c_ctx)
        mod = _adaln(cv, w_mod[layer], b_mod[layer]).reshape(SUBLANES, N_MOD, d)
        mod_x = mod[:bsz]
        mod_c = jnp.broadcast_to(mod[bsz:bsz + 1], (bsz, N_MOD, d))

        ffn = functools.partial(_ffn, wg=wg_all, wu=wu_all, wd=wd_all, layer=layer)

        h = _prenorm(x, norm_pre[layer, 0], mod_x[:, 0], mod_x[:, 1])
        y = ffn(h.reshape(bsz * n, d), slot=0)
        x, hx = _resid(x, y, norm_post[layer, 0], mod_x[:, 2], 0.5,
                       nxt=(norm_pre[layer, 1], mod_x[:, 3], mod_x[:, 4]))
        hcx = _prenorm(ctx, norm_pre[layer, 0], mod_c[:, 0], mod_c[:, 1])
        yc = ffn(hcx.reshape(bsz * lc, d), slot=0)
        ctx, hc = _resid(ctx, yc, norm_post[layer, 0], mod_c[:, 2], 0.5,
                         nxt=(norm_pre[layer, 1], mod_c[:, 3], mod_c[:, 4]))

        w_dt = jnp.pad(w_in_all[layer:layer + 1, :, dt_col0:], ((0, 0), (0, 0), (0, LANES - 2 * heads)))
        hx2, hc2 = hx.reshape(bsz * n, d), hc.reshape(bsz * lc, d)
        a_lat = _proj(hx2, w_in_all, layer, BF16, n_cols=xbc_col0,
                      rope=(rope_tabs, n, (0, attn_w), (ctx_col0, ctx_col0 + kv_w))).reshape(bsz, n, xbc_col0)
        xbc_lat = _proj(hx2, w_in_all, layer, BF16, col0=xbc_col0, n_cols=xbc_w).reshape(bsz, n, xbc_w)
        dt_lat = _proj(hx2, w_dt, 0, F32).reshape(bsz, n, LANES)
        kv_ctx = _proj(hc2, w_in_all, layer, BF16, col0=ctx_col0, n_cols=2 * kv_w).reshape(bsz, lc, 2 * kv_w)
        xbc_ctx = _proj(hc2, w_in_all, layer, BF16, col0=xbc_col0, n_cols=xbc_w).reshape(bsz, lc, xbc_w)
        dt_ctx = _proj(hc2, w_dt, 0, F32).reshape(bsz, lc, LANES)

        attn = _attention(a_lat, kv_ctx, attn_sink[layer].astype(F32), attn_w, ctx_col0, ctx_col0 + kv_w)
        xbc_lat = _conv_silu(xbc_lat, conv_w[layer], conv_b[layer])
        xbc_ctx = _conv_silu(xbc_ctx, conv_w[layer], conv_b[layer])
        y_f, y_b = _ssd(xbc_lat, dt_lat, xbc_ctx, dt_ctx, a_log[layer], dt_bias[layer], inner, heads)
        d_exp = jnp.repeat(d_skip[layer].astype(F32), SSM_HEAD_DIM)
        merged = _merge(attn, y_f, y_b, xbc_lat, a_lat, attn_w, d_exp, attn_norm[layer], ssm_norm[layer])
        y = _proj(merged.reshape(bsz * n, attn_w + inner), w_out_all, layer, BF16, tn=512)
        x, h = _resid(x, y, norm_post[layer, 1], mod_x[:, 5], 1.0,
                      nxt=(norm_pre[layer, 2], mod_x[:, 6], mod_x[:, 7]))

        y = ffn(h.reshape(bsz * n, d), slot=1)
        x = _resid(x, y, norm_post[layer, 2], mod_x[:, 8], 0.5)
        if not last:
            raise NotImplementedError("context output path is only needed for depth > 1")
    return x
```

```python
import functools

import jax
import jax.numpy as jnp
from jax import lax
from jax.experimental import pallas as pl
from jax.experimental.pallas import tpu as pltpu

GRID_W = 64
HEAD_DIM = 128
N_HEADS = 16
N_KV_HEADS = 4
GQA_GROUP = N_HEADS // N_KV_HEADS
BLOCK = 128
ROPE_BASE = 10000.0
ROPE_PAIRS = HEAD_DIM // 4
SSM_HEAD_DIM = 64
SSM_GROUPS = 4
SSM_STATE = 128
CONV_K = 5
CHUNK = 128
N_MOD = 9
EPS = 1e-6
NEG_INF = -1e30
LOG2_E = 1.4426950408889634

LANES = 128
SUBLANES = 8
MIB = 1024 * 1024

F32 = jnp.float32
BF16 = jnp.bfloat16


def _params(semantics, vmem_mib):
    return pltpu.CompilerParams(dimension_semantics=semantics, vmem_limit_bytes=vmem_mib * MIB)


def _stream_spec(block, index_map):
    return pl.BlockSpec(block, index_map)


def _silu(t):
    return t * jax.nn.sigmoid(t)


def _rms_scale(t):
    return lax.rsqrt(jnp.mean(t * t, axis=-1, keepdims=True) + EPS)


def _mod_kernel(c_ref, w_ref, b_ref, o_ref):
    s = _silu(c_ref[...]).astype(BF16)
    o_ref[...] = jnp.dot(s, w_ref[...].astype(BF16), preferred_element_type=F32) + b_ref[...]


def _adaln(cv, w_mod, b_mod):
    rows, d = cv.shape
    n = w_mod.shape[1]
    tn = 1024
    return pl.pallas_call(
        _mod_kernel,
        grid=(n // tn,),
        in_specs=[
            pl.BlockSpec((rows, d), lambda j: (0, 0)),
            _stream_spec((d, tn), lambda j: (0, j)),
            pl.BlockSpec((1, tn), lambda j: (0, j)),
        ],
        out_specs=pl.BlockSpec((rows, tn), lambda j: (0, j)),
        out_shape=jax.ShapeDtypeStruct((rows, n), F32),
        compiler_params=_params(("parallel",), 48),
        name="adaln",
    )(cv, w_mod, b_mod.reshape(1, n))


def _prenorm_kernel(s_ref, g_ref, sh_ref, sc_ref, h_ref):
    s = s_ref[0]
    h = s * _rms_scale(s) * g_ref[...]
    h_ref[0] = (h * (1.0 + sc_ref[0]) + sh_ref[0]).astype(BF16)


def _prenorm(s, g_pre, shift, scale, tr=512):
    b, n, d = s.shape
    tr = min(tr, n)
    row_map = lambda i, r: (i, r, 0)
    vec = pl.BlockSpec((1, 1, d), lambda i, r: (i, 0, 0))
    return pl.pallas_call(
        _prenorm_kernel,
        grid=(b, n // tr),
        in_specs=[_stream_spec((1, tr, d), row_map), pl.BlockSpec((1, d), lambda i, r: (0, 0)), vec, vec],
        out_specs=pl.BlockSpec((1, tr, d), row_map),
        out_shape=jax.ShapeDtypeStruct((b, n, d), BF16),
        compiler_params=_params(("parallel", "parallel"), 40),
        name="prenorm",
    )(s, g_pre.reshape(1, d), shift.reshape(b, 1, d), scale.reshape(b, 1, d))


def _resid_kernel(*refs, coef, with_h):
    if with_h:
        s_ref, y_ref, gp_ref, gate_ref, gn_ref, sh_ref, sc_ref, o_ref, h_ref = refs
    else:
        s_ref, y_ref, gp_ref, gate_ref, o_ref = refs
    y = y_ref[0].astype(F32)
    s_new =s_ref[0] + (coef * gate_ref[0]) * (y * _rms_scale(y) * gp_ref[...])
    o_ref[0] = s_new
    if with_h:
        h = s_new * _rms_scale(s_new) * gn_ref[...]
        h_ref[0] = (h * (1.0 + sc_ref[0]) + sh_ref[0]).astype(BF16)


def _resid(s, y, g_post, gate, coef, nxt=None):
    b, n, d = s.shape
    tr = min(n, 256 if nxt is not None else 512)
    row_map = lambda i, r: (i, r, 0)
    row = pl.BlockSpec((1, tr, d), row_map)
    vec = pl.BlockSpec((1, 1, d), lambda i, r: (i, 0, 0))
    par = pl.BlockSpec((1, d), lambda i, r: (0, 0))
    args = [s, y.reshape(b, n, d), g_post.reshape(1, d), gate.reshape(b, 1, d)]
    in_specs = [_stream_spec((1, tr, d), row_map), _stream_spec((1, tr, d), row_map), par, vec]
    out_shape = [jax.ShapeDtypeStruct((b, n, d), F32)]
    out_specs = [row]
    if nxt is not None:
        g_next, shift, scale = nxt
        args += [g_next.reshape(1, d), shift.reshape(b, 1, d), scale.reshape(b, 1, d)]
        in_specs += [par, vec, vec]
        out_shape.append(jax.ShapeDtypeStruct((b, n, d), BF16))
        out_specs.append(row)
    out = pl.pallas_call(
        functools.partial(_resid_kernel, coef=coef, with_h=nxt is not None),
        grid=(b, n // tr),
        in_specs=in_specs,
        out_specs=out_specs,
        out_shape=out_shape,
        compiler_params=_params(("parallel", "parallel"), 56),
        name="resid",
    )(*args)
    return out if nxt is not None else out[0]


def _ffn_kernel(h_ref, wg_ref, wu_ref, wd_ref, y_ref, acc_ref, *, n_chunk):
    j = pl.program_id(1)

    @pl.when(j == 0)
    def _():
        acc_ref[...] = jnp.zeros_like(acc_ref)

    h = h_ref[...]
    g = jnp.dot(h, wg_ref[...].astype(BF16), preferred_element_type=F32)
    u = jnp.dot(h, wu_ref[...].astype(BF16), preferred_element_type=F32)
    a = (_silu(g) * u).astype(BF16)
    d = acc_ref.shape[1]
    for n0 in range(0, d, n_chunk):
        wd = wd_ref[:, n0:n0 + n_chunk].astype(BF16)
        acc_ref[:, n0:n0 + n_chunk] += jnp.dot(a, wd, preferred_element_type=F32)

    @pl.when(j == pl.num_programs(1) - 1)
    def _():
        y_ref[...] = acc_ref[...].astype(y_ref.dtype)


def _ffn(h, wg, wu, wd, layer, slot, tm=1024, tf=256):
    r, d = h.shape
    f = wg.shape[3]
    tm = min(tm, r)
    once = pl.Buffered(1)
    return pl.pallas_call(
        functools.partial(_ffn_kernel, n_chunk=1024),
        grid=(r // tm, f // tf),
        in_specs=[
            pl.BlockSpec((tm, d), lambda i, j: (i, 0), pipeline_mode=once),
            pl.BlockSpec((None, None, d, tf), lambda i, j: (layer, slot, 0, j)),
            pl.BlockSpec((None, None, d, tf), lambda i, j: (layer, slot, 0, j)),
            pl.BlockSpec((None, None, tf, d), lambda i, j: (layer, slot, j, 0)),
        ],
        out_specs=pl.BlockSpec((tm, d), lambda i, j: (i, 0), pipeline_mode=once),
        out_shape=jax.ShapeDtypeStruct((r, d), BF16),
        scratch_shapes=[pltpu.VMEM((tm, d), F32)],
        compiler_params=_params(("parallel", "arbitrary"), 60),
        name="ffn",
    )(h, wg, wu, wd)


ROPE_SLAB = 2 * HEAD_DIM


def _proj_kernel(*refs, tile_kinds):
    if tile_kinds is None:
        h_ref, w_ref, o_ref = refs
        o_ref[...] = jnp.dot(h_ref[...], w_ref[...].astype(BF16), preferred_element_type=F32).astype(o_ref.dtype)
        return
    h_ref, w_ref, cos_ref, sa_ref, sb_ref, o_ref = refs
    j = pl.program_id(1)

    for modes, tiles in tile_kinds:
        @pl.when(functools.reduce(jnp.logical_or, [j == t for t in tiles]))
        def _(modes=modes):
            if all(m is None for m in modes):
                o_ref[...] = jnp.dot(h_ref[...], w_ref[...].astype(BF16),
                                     preferred_element_type=F32).astype(o_ref.dtype)
                return
            tabs = {"k": (cos_ref[...], sa_ref[...], sb_ref[...])}
            if "q" in modes:
                tabs["q"] = tuple(t * (LOG2_E * HEAD_DIM ** -0.5) for t in tabs["k"])
            h = h_ref[...]
            for s, mode in enumerate(modes):
                c0 = s * ROPE_SLAB
                res = jnp.dot(h, w_ref[:, c0:c0 + ROPE_SLAB].astype(BF16), preferred_element_type=F32)
                if mode is None:
                    o_ref[:, c0:c0 + ROPE_SLAB] = res.astype(o_ref.dtype)
                    continue
                cos, sa, sb = tabs[mode]
                for k in range(ROPE_SLAB // HEAD_DIM):
                    t = res[:, k * HEAD_DIM:(k + 1) * HEAD_DIM]
                    o = t * cos + pltpu.roll(t, ROPE_PAIRS, 1) * sa + pltpu.roll(t, HEAD_DIM - ROPE_PAIRS, 1) * sb
                    o_ref[:, c0 + k * HEAD_DIM:c0 + (k + 1) * HEAD_DIM] = o.astype(o_ref.dtype)


def _proj(h, w, layer, out_dtype, *, col0=0, n_cols=None, tm=1024, tn=1024, rope=None):
    r, kdim = h.shape
    n_cols = w.shape[2] - col0 if n_cols is None else n_cols
    tm = min(tm, r)
    tn = min(tn, n_cols)
    jo = col0 // tn
    in_specs = [
        pl.BlockSpec((tm, kdim), lambda i, j: (i, 0)),
        pl.BlockSpec((None, kdim, tn), lambda i, j: (layer, 0, j + jo)),
    ]
    args = [h, w]
    tile_kinds = None
    if rope is not None:
        tables, seq_len, q_cols, k_cols = rope
        per_seq = seq_len // tm
        tab = pl.BlockSpec((tm, HEAD_DIM), lambda i, j: (i % per_seq, 0))
        in_specs += [tab, tab, tab]
        args += list(tables)

        def mode(c):
            return "q" if q_cols[0] <= c < q_cols[1] else "k" if k_cols[0] <= c < k_cols[1] else None

        kinds = {}
        for t in range(n_cols // tn):
            modes = tuple(mode(col0 + t * tn + c) for c in range(0, tn, ROPE_SLAB))
            kinds.setdefault(modes, []).append(t)
        tile_kinds = tuple((m, tuple(ts)) for m, ts in kinds.items())
    return pl.pallas_call(
        functools.partial(_proj_kernel, tile_kinds=tile_kinds),
        grid=(r // tm, n_cols // tn),
        in_specs=in_specs,
        out_specs=pl.BlockSpec((tm, tn), lambda i, j: (i, j)),
        out_shape=jax.ShapeDtypeStruct((r, n_cols), out_dtype),
        compiler_params=_params(("parallel", "arbitrary"), 52),
        name="proj",
    )(*args)


def _rope_tables(n):
    row = (jnp.arange(n) // GRID_W).astype(F32)
    col = (jnp.arange(n) % GRID_W).astype(F32)
    inv = ROPE_BASE ** (-jnp.arange(ROPE_PAIRS, dtype=F32) / ROPE_PAIRS)
    ar = row[:, None] * inv
    ac = col[:, None] * inv
    ang = jnp.concatenate([ar, ar, ac, ac], axis=-1)
    cos, sin = jnp.cos(ang), jnp.sin(ang)
    upper = (jnp.arange(HEAD_DIM) % (2 * ROPE_PAIRS)) >= ROPE_PAIRS
    return cos, jnp.where(upper, sin, 0.0), jnp.where(upper, 0.0, -sin)


def _conv_kernel(x_ref, w_ref, b_ref, o_ref, pad_ref, *, rows):
    n = x_ref.shape[1]
    tc = x_ref.shape[2]
    halo = CONV_K // 2
    zeros = jnp.zeros((SUBLANES, tc), F32)
    pad_ref[0:SUBLANES, :] = zeros
    pad_ref[n + SUBLANES:n + 2 * SUBLANES, :] = zeros
    pad_ref[SUBLANES:n + SUBLANES, :] = x_ref[0].astype(F32)
    w = w_ref[...]
    bias = b_ref[...]
    for r0 in range(0, n, rows):
        acc = jnp.broadcast_to(bias, (rows, tc))
        for k in range(CONV_K):
            start = SUBLANES + r0 + k - halo
            acc = acc + pad_ref[start:start + rows, :] * w[k:k + 1, :]
        o_ref[0, r0:r0 + rows, :] = _silu(acc).astype(o_ref.dtype)


def _conv_silu(u, w, bias, tc=128):
    b, n, c = u.shape
    rows = min(512, n)
    return pl.pallas_call(
        functools.partial(_conv_kernel, rows=rows),
        grid=(b, c // tc),
        in_specs=[
            pl.BlockSpec((1, n, tc), lambda i, j: (i, 0, j)),
            pl.BlockSpec((CONV_K, tc), lambda i, j: (0, j)),
            pl.BlockSpec((1, tc), lambda i, j: (0, j)),
        ],
        out_specs=pl.BlockSpec((1, n, tc), lambda i, j: (i, 0, j)),
        out_shape=jax.ShapeDtypeStruct((b, n, c), BF16),
        scratch_shapes=[pltpu.VMEM((n + 2 * SUBLANES, tc), F32)],
        compiler_params=_params(("parallel", "parallel"), 40),
        name="conv_silu",
    )(u, w, bias.reshape(1, c))


def _attn_kernel(sink_ref, q_ref, kp_ref, ko_ref, kn_ref, vp_ref, vo_ref, vn_ref, kc_ref, vc_ref, o_ref):
    n = pl.program_id(1)
    nb = pl.num_programs(1)
    cols_q = GQA_GROUP * BLOCK
    kj = lax.broadcasted_iota(jnp.int32, (BLOCK, cols_q), 0)
    qi = lax.broadcasted_iota(jnp.int32, (BLOCK, cols_q), 1) & (BLOCK - 1)
    prev_ok = kj >= qi + jnp.where(n > 0, 0, BLOCK)
    next_ok = kj <= qi - jnp.where(n < nb - 1, 0, BLOCK)
    for bi, kv in [(bi, kv) for bi in range(o_ref.shape[0]) for kv in range(N_KV_HEADS)]:
        heads = range(kv * GQA_GROUP, (kv + 1) * GQA_GROUP)
        cols = slice(kv * HEAD_DIM, (kv + 1) * HEAD_DIM)
        qs = jnp.concatenate([q_ref[bi, :, h * HEAD_DIM:(h + 1) * HEAD_DIM] for h in heads], axis=0)
        kb = jnp.concatenate([r[bi, :, cols] for r in (kp_ref, ko_ref, kn_ref, kc_ref)], axis=0)
        vb = jnp.concatenate([r[bi, :, cols] for r in (vp_ref, vo_ref, vn_ref, vc_ref)], axis=0)
        s = lax.dot_general(kb, qs, (((1,), (1,)), ((), ())), preferred_element_type=F32)
        s = jnp.concatenate([
            jnp.where(prev_ok, s[0:BLOCK], NEG_INF), s[BLOCK:2 * BLOCK],
            jnp.where(next_ok, s[2 * BLOCK:3 * BLOCK], NEG_INF), s[3 * BLOCK:]], axis=0)
        sink = jnp.concatenate([jnp.full((1, BLOCK), sink_ref[h] * LOG2_E, F32) for h in heads], axis=1)
        m = jnp.maximum(jnp.max(s, axis=0, keepdims=True), sink)
        p = jnp.exp2(s - m)
        denom = jnp.sum(p, axis=0, keepdims=True) + jnp.exp2(sink - m)
        o_t = lax.dot_general(vb, p.astype(BF16), (((0,), (0,)), ((), ())), preferred_element_type=F32) / denom
        for g, h in enumerate(heads):
            o_ref[bi, :, h * HEAD_DIM:(h + 1) * HEAD_DIM] = o_t[:, g * BLOCK:(g + 1) * BLOCK].T.astype(o_ref.dtype)


def _attention(a, kvc, sink, q_width, k_col0, v_col0):
    b, n, _ = a.shape
    lc = kvc.shape[1]
    nb = n // BLOCK
    kw = N_KV_HEADS * HEAD_DIM
    kb0, vb0 = k_col0 // kw, v_col0 // kw

    bb = b

    def kv_spec(col_blk, shift):
        return pl.BlockSpec((bb, BLOCK, kw), lambda i, j: (i, jnp.clip(j + shift, 0, nb - 1), col_blk))

    return pl.pallas_call(
        _attn_kernel,
        grid=(b // bb, nb),
        in_specs=[
            pl.BlockSpec(memory_space=pltpu.SMEM),
            pl.BlockSpec((bb, BLOCK, q_width), lambda i, j: (i, j, 0)),
            kv_spec(kb0, -1), kv_spec(kb0, 0), kv_spec(kb0, 1),
            kv_spec(vb0, -1), kv_spec(vb0, 0), kv_spec(vb0, 1),
            pl.BlockSpec((bb, lc, kw), lambda i, j: (i, 0, 0)),
            pl.BlockSpec((bb, lc, kw), lambda i, j: (i, 0, 1)),
        ],
        out_specs=pl.BlockSpec((bb, BLOCK, q_width), lambda i, j: (i, j, 0)),
        out_shape=jax.ShapeDtypeStruct((b, n, q_width), BF16),
        compiler_params=_params(("parallel", "arbitrary"), 32),
        name="window_attn",
    )(sink, a, a, a, a, a, a, a, kvc, kvc)


def _split_bf16(t, parts):
    out = []
    for _ in range(parts):
        hi = t.astype(BF16)
        out.append(hi)
        t = t - hi.astype(F32)
    return out


def _ssd_kernel(*refs, n_ctx_chunks, heads, inner):
    ins, consts, outs, scratch = refs[:16], refs[16:19], refs[19:21], refs[21:]
    step = pl.program_id(1)

    @pl.when(step == 0)
    def _():
        scratch[-1][...] = jnp.zeros_like(scratch[-1])

    def stage(srcs):
        for bi in range(outs[0].shape[0]):
            for d in range(2):
                for k in range(4):
                    scratch[k][bi, d] = srcs[4 * d + k][bi]

    pl.when(step < n_ctx_chunks)(lambda: stage(ins[8:16]))
    pl.when(step >= n_ctx_chunks)(lambda: stage(ins[0:8]))

    for bi in range(outs[0].shape[0]):
        _ssd_step(*consts, *(r.at[bi] for r in outs), *(r.at[bi] for r in scratch), heads=heads, inner=inner)


def _ssd_step(alog_ref, bias_ref, e_ref, yf_ref, yb_ref, x_s, b_s, c_s, dt_s, h_s, *, heads, inner):
    hpg = heads // SSM_GROUPS
    gw = hpg * SSM_HEAD_DIM

    lane = lax.broadcasted_iota(jnp.int32, (CHUNK, LANES), 1)
    row = lax.broadcasted_iota(jnp.int32, (CHUNK, CHUNK), 0)
    col = lax.broadcasted_iota(jnp.int32, (CHUNK, CHUNK), 1)
    is_fwd = lane < heads
    live = lane < 2 * heads

    dt_raw = jnp.where(is_fwd, dt_s[0], dt_s[1])
    dtv = jax.nn.softplus(dt_raw + bias_ref[...])
    a_row = jnp.where(live[0:1], -jnp.exp(alog_ref[...]), 0.0)
    da = dtv * a_row
    tri = jnp.where(row >= col, 1.0, 0.0).astype(BF16)
    acs = sum(jnp.dot(tri, part, preferred_element_type=F32) for part in _split_bf16(da, 3))
    tot = acs[CHUNK - 1:CHUNK, :]
    u = acs - jnp.where(is_fwd, 0.0, da)
    e_u = jnp.exp(u)
    e_r = jnp.exp(tot - u)
    rs = jnp.where(is_fwd, e_u, e_r)
    wst = jnp.where(is_fwd, e_r, e_u) * dtv
    packed = jnp.where(lane < 2 * heads, rs, pltpu.roll(wst, 2 * heads, 1))
    hi, lo = _split_bf16(packed, 2)
    packed2 = jnp.concatenate([hi, lo], axis=1)

    def expand(k):
        return jnp.dot(packed2, e_ref[:, k * inner:(k + 1) * inner], preferred_element_type=F32)

    us2 = jnp.where(is_fwd, u, -u) * LOG2_E
    col_t = (us2 - jnp.log2(dtv)).T

    for d in range(2):
        rs_x = expand(d)
        w_x = expand(2 + d)
        dec = rs_x[CHUNK - 1:CHUNK, :] if d == 0 else rs_x[0:1, :]
        keep = (row >= col) if d == 0 else (col >= row)
        y_ref = yf_ref if d == 0 else yb_ref
        for g in range(SSM_GROUPS):
            bm = b_s[d, :, g * SSM_STATE:(g + 1) * SSM_STATE]
            cm = c_s[d, :, g * SSM_STATE:(g + 1) * SSM_STATE]
            xg = x_s[d, :, g * gw:(g + 1) * gw]
            cb = lax.dot_general(cm, bm, (((1,), (1,)), ((), ())), preferred_element_type=F32)
            h_old = h_s[d, :, g * gw:(g + 1) * gw]
            y_off = jnp.dot(cm, h_old.astype(BF16), preferred_element_type=F32) * rs_x[:, g * gw:(g + 1) * gw]
            pieces = []
            for r in range(0, hpg, 2):
                mats = []
                for hh in (g * hpg + r, g * hpg + r + 1):
                    ln = hh + d * heads
                    diff = us2[:, ln:ln + 1] - col_t[ln:ln + 1, :]
                    decay_dt = jnp.exp2(jnp.where(keep, diff, -jnp.inf))
                    mats.append((cb * decay_dt).astype(BF16))
                xp = xg[:, r * SSM_HEAD_DIM:(r + 2) * SSM_HEAD_DIM]
                left = lane < SSM_HEAD_DIM
                zero = jnp.zeros_like(xp)
                rhs = jnp.concatenate([jnp.where(left, xp, zero), jnp.where(left, zero, xp)], axis=0)
                pieces.append(jnp.dot(jnp.concatenate(mats, axis=1), rhs, preferred_element_type=F32))
            y_g = jnp.concatenate(pieces, axis=1) + y_off
            y_ref[:, g * gw:(g + 1) * gw] = y_g.astype(y_ref.dtype)

            xw = (xg.astype(F32) * w_x[:, g * gw:(g + 1) * gw]).astype(BF16)
            st = lax.dot_general(bm, xw, (((0,), (0,)), ((), ())), preferred_element_type=F32)
            h_s[d, :, g * gw:(g + 1) * gw] = h_old * dec[:, g * gw:(g + 1) * gw] + st


def _ssd(xbc_lat, dt_lat, xbc_ctx, dt_ctx, a_log, dt_bias, inner, heads):
    b, n, _ = xbc_lat.shape
    lc = xbc_ctx.shape[1]
    nlat, ncc = n // CHUNK, lc // CHUNK
    bc = SSM_GROUPS * SSM_STATE
    assert 4 * heads == LANES and inner % bc == 0
    b_blk, c_blk = inner // bc, inner // bc + 1

    def lat_f(s):
        return jnp.maximum(s - ncc, 0)

    def lat_b(s):
        return nlat - 1 - jnp.maximum(s - ncc, 0)

    def ctx_f(s):
        return jnp.minimum(s, ncc - 1)

    def ctx_b(s):
        return jnp.maximum(ncc - 1 - s, 0)

    bb = b

    def specs(chunk_of):
        return [
            pl.BlockSpec((bb, CHUNK, inner), lambda i, s: (i, chunk_of(s), 0)),
            pl.BlockSpec((bb, CHUNK, bc), lambda i, s: (i, chunk_of(s), b_blk)),
            pl.BlockSpec((bb, CHUNK, bc), lambda i, s: (i, chunk_of(s), c_blk)),
        ]

    def dt_spec(chunk_of):
        return [pl.BlockSpec((bb, CHUNK, LANES), lambda i, s: (i, chunk_of(s), 0))]

    in_specs = (specs(lat_f) + dt_spec(lat_f) + specs(lat_b) + dt_spec(lat_b)
                + specs(ctx_f) + dt_spec(ctx_f) + specs(ctx_b) + dt_spec(ctx_b))
    const = lambda shape: pl.BlockSpec(shape, lambda i, s: (0, 0))
    in_specs += [const((1, LANES)), const((1, LANES)), const((2 * LANES, 4 * inner))]

    pad = jnp.zeros((LANES - 2 * heads,), F32)
    alog_row = jnp.concatenate([a_log.reshape(-1).astype(F32), pad]).reshape(1, LANES)
    bias_row = jnp.concatenate([dt_bias.reshape(-1).astype(F32), pad]).reshape(1, LANES)
    cidx = jnp.arange(4 * inner)
    src = heads * (cidx // inner) + (cidx % inner) // SSM_HEAD_DIM
    expand = (jnp.arange(LANES)[:, None] == src[None, :]).astype(BF16)
    expand = jnp.concatenate([expand, expand], axis=0)

    lat = [xbc_lat, xbc_lat, xbc_lat, dt_lat]
    ctx = [xbc_ctx, xbc_ctx, xbc_ctx, dt_ctx]
    y_shape = jax.ShapeDtypeStruct((b, n, inner), BF16)
    return pl.pallas_call(
        functools.partial(_ssd_kernel, n_ctx_chunks=ncc, heads=heads, inner=inner),
        grid=(b // bb, ncc + nlat),
        in_specs=in_specs,
        out_specs=[
            pl.BlockSpec((bb, CHUNK, inner), lambda i, s: (i, lat_f(s), 0)),
            pl.BlockSpec((bb, CHUNK, inner), lambda i, s: (i, lat_b(s), 0)),
        ],
        out_shape=[y_shape, y_shape],
        scratch_shapes=[
            pltpu.VMEM((bb, 2, CHUNK, inner), BF16),
            pltpu.VMEM((bb, 2, CHUNK, bc), BF16),
            pltpu.VMEM((bb, 2, CHUNK, bc), BF16),
            pltpu.VMEM((bb, 2, CHUNK, LANES), F32),
            pltpu.VMEM((bb, 2, SSM_STATE, inner), F32),
        ],
        compiler_params=_params(("parallel", "arbitrary"), 48),
        name="ssd_scan",
    )(*lat, *lat, *ctx, *ctx, alog_row, bias_row, expand)


def _merge_kernel(attn_ref, yf_ref, yb_ref, xs_ref, z_ref, d_ref, an_ref, sn_ref, o_ref):
    aw = attn_ref.shape[2]
    a = attn_ref[0].astype(F32)
    o_ref[0, :, 0:aw] = (a * _rms_scale(a) * an_ref[...]).astype(BF16)
    y = yf_ref[0].astype(F32) + yb_ref[0].astype(F32) + d_ref[...] * xs_ref[0].astype(F32)
    t = y * _silu(z_ref[0].astype(F32))
    o_ref[0, :, aw:] = (t * _rms_scale(t) * sn_ref[...]).astype(BF16)


def _merge(attn, y_f, y_b, xbc, a, z_col0, d_exp, attn_norm, ssm_norm, tr=512):
    b, n, aw = attn.shape
    inner = y_f.shape[2]
    row = lambda w, cb=0: _stream_spec((1, tr, w), lambda i, r: (i, r, cb))
    par = lambda w: pl.BlockSpec((1, w), lambda i, r: (0, 0))
    return pl.pallas_call(
        _merge_kernel,
        grid=(b, n // tr),
        in_specs=[row(aw), row(inner), row(inner), row(inner), row(inner, z_col0 // inner),
                  par(inner), par(aw), par(inner)],
        out_specs=pl.BlockSpec((1, tr, aw + inner), lambda i, r: (i, r, 0)),
        out_shape=jax.ShapeDtypeStruct((b, n, aw + inner), BF16),
        compiler_params=_params(("parallel", "parallel"), 48),
        name="merge",
    )(attn, y_f, y_b, xbc, a, d_exp.reshape(1, inner), attn_norm.reshape(1, aw), ssm_norm.reshape(1, inner))


def kernel(x, c, ctx, c_ctx, w_mod, b_mod, norm_pre, norm_post, w_ffn_gate, w_ffn_up, w_ffn_down,
           w_in, attn_sink, attn_norm, conv_w, conv_b, a_log, dt_bias, d_skip, ssm_norm, w_out):
    bsz, n, d = x.shape
    lc = ctx.shape[1]
    depth = w_mod.shape[0]
    attn_w = N_HEADS * HEAD_DIM
    inner = d // 2
    heads = inner // SSM_HEAD_DIM
    kv_w = N_KV_HEADS * HEAD_DIM
    bc_w = SSM_GROUPS * SSM_STATE
    xbc_w = inner + 2 * bc_w
    ctx_col0 = attn_w + inner
    xbc_col0 = ctx_col0 + 2 * kv_w
    dt_col0 = xbc_col0 + xbc_w
    rope_tabs = _rope_tables(n)
    wg_all, wu_all, wd_all = w_ffn_gate, w_ffn_up, w_ffn_down
    w_in_all, w_out_all = w_in.astype(BF16), w_out

    for layer in range(depth):
        last = layer == depth - 1
        cv = jnp.zeros((SUBLANES, d), F32).at[:bsz].set(c).at[bsz].set(c_ctx)
        mod = _adaln(cv, w_mod[layer], b_mod[layer]).reshape(SUBLANES, N_MOD, d)
        mod_x = mod[:bsz]
        mod_c = jnp.broadcast_to(mod[bsz:bsz + 1], (bsz, N_MOD, d))

        ffn = functools.partial(_ffn, wg=wg_all, wu=wu_all, wd=wd_all, layer=layer)

        h = _prenorm(x, norm_pre[layer, 0], mod_x[:, 0], mod_x[:, 1])
        y = ffn(h.reshape(bsz * n, d), slot=0)
        x, hx = _resid(x, y, norm_post[layer, 0], mod_x[:, 2], 0.5,
                       nxt=(norm_pre[layer, 1], mod_x[:, 3], mod_x[:, 4]))
        hcx = _prenorm(ctx, norm_pre[layer, 0], mod_c[:, 0], mod_c[:, 1])
        yc = ffn(hcx.reshape(bsz * lc, d), slot=0)
        ctx, hc = _resid(ctx, yc, norm_post[layer, 0], mod_c[:, 2], 0.5,
                         nxt=(norm_pre[layer, 1], mod_c[:, 3], mod_c[:, 4]))

        w_dt = jnp.pad(w_in_all[layer:layer + 1, :, dt_col0:], ((0, 0), (0, 0), (0, LANES - 2 * heads)))
        hx2, hc2 = hx.reshape(bsz * n, d), hc.reshape(bsz * lc, d)
        a_lat = _proj(hx2, w_in_all, layer, BF16, n_cols=xbc_col0,
                      rope=(rope_tabs, n, (0, attn_w), (ctx_col0, ctx_col0 + kv_w))).reshape(bsz, n, xbc_col0)
        xbc_lat = _proj(hx2, w_in_all, layer, BF16, col0=xbc_col0, n_cols=xbc_w).reshape(bsz, n, xbc_w)
        dt_lat = _proj(hx2, w_dt, 0, F32).reshape(bsz, n, LANES)
        kv_ctx = _proj(hc2, w_in_all, layer, BF16, col0=ctx_col0, n_cols=2 * kv_w).reshape(bsz, lc, 2 * kv_w)
        xbc_ctx = _proj(hc2, w_in_all, layer, BF16, col0=xbc_col0, n_cols=xbc_w).reshape(bsz, lc, xbc_w)
        dt_ctx = _proj(hc2, w_dt, 0, F32).reshape(bsz, lc, LANES)

        attn = _attention(a_lat, kv_ctx, attn_sink[layer].astype(F32), attn_w, ctx_col0, ctx_col0 + kv_w)
        xbc_lat = _conv_silu(xbc_lat, conv_w[layer], conv_b[layer])
        xbc_ctx = _conv_silu(xbc_ctx, conv_w[layer], conv_b[layer])
        y_f, y_b = _ssd(xbc_lat, dt_lat, xbc_ctx, dt_ctx, a_log[layer], dt_bias[layer], inner, heads)
        d_exp = jnp.repeat(d_skip[layer].astype(F32), SSM_HEAD_DIM)
        merged = _merge(attn, y_f, y_b, xbc_lat, a_lat, attn_w, d_exp, attn_norm[layer], ssm_norm[layer])
        y = _proj(merged.reshape(bsz * n, attn_w + inner), w_out_all, layer, BF16, tn=512)
        x, h = _resid(x, y, norm_post[layer, 1], mod_x[:, 5], 1.0,
                      nxt=(norm_pre[layer, 2], mod_x[:, 6], mod_x[:, 7]))

        y = ffn(h.reshape(bsz * n, d), slot=1)
        x = _resid(x, y, norm_post[layer, 2], mod_x[:, 8], 0.5)
        if not last:
            raise NotImplementedError("context output path is only needed for depth > 1")
    return x
```
